```python
import jax, jax.numpy as jnp
from jax import lax
import numpy as np

D_MODEL = 2048
BATCH = 8
SEQ = 2048
DEPTH = 4

N_MIXERS = 3
GRID_W = 64
HEAD_DIM = 128
EPS = 1e-6
NEG_INF = -1e30
NA_HEADS = D_MODEL // HEAD_DIM
NA_WIN_R = 8
NA_WIN_C = 16
SC_WIDTH = 3
GQA_Q_HEADS = D_MODEL // HEAD_DIM
GQA_KV_HEADS = GQA_Q_HEADS // 4
GQA_GROUP = GQA_Q_HEADS // GQA_KV_HEADS
Q_BLOCK = 128
ROPE_THETA = 10000.0
D_FF = ((8 * D_MODEL // 3 + 255) // 256) * 256
FFN_CONV_WIDTH = 3

kernel_name = "hybrid_natten_shortconv_gqa_convffn_encoder"


def rms_norm(x, g):
    xf = x.astype(jnp.float32)
    y = xf * lax.rsqrt(jnp.mean(xf * xf, axis=-1, keepdims=True) + EPS)
    return (y * g.astype(jnp.float32)).astype(x.dtype)


def dwconv_centered(x, w):
    k = w.shape[0]
    pad = k // 2
    s = x.shape[1]
    xp = jnp.pad(x, ((0, 0), (pad, k - 1 - pad), (0, 0)))
    out = xp[:, 0:s] * w[0]
    for i in range(1, k):
        out = out + xp[:, i:i + s] * w[i]
    return out


def neighborhood_attention(x, w_qkv, rpb, w_o):
    b, s, _ = x.shape
    rows = s // GRID_W
    wr = min(NA_WIN_R, rows)
    q, k, v = jnp.split(x @ w_qkv, 3, axis=-1)
    grid = lambda t: t.reshape(b, rows, GRID_W, NA_HEADS, HEAD_DIM)
    q = grid(q) * (HEAD_DIM ** -0.5)
    k, v = grid(k), grid(v)
    cols = jnp.arange(GRID_W)
    c_start = jnp.clip(cols - NA_WIN_C // 2, 0, GRID_W - NA_WIN_C)
    col_valid = (cols[None, :] >= c_start[:, None]) & (cols[None, :] < c_start[:, None] + NA_WIN_C)
    dc = jnp.clip(cols[None, :] - cols[:, None] + NA_WIN_C - 1, 0, 2 * NA_WIN_C - 2)
    rpb_cols = rpb[:, :, dc]

    def row_block(r):
        r_start = jnp.clip(r - NA_WIN_R // 2, 0, rows - wr)
        k_r = lax.dynamic_slice_in_dim(k, r_start, wr, axis=1)
        v_r = lax.dynamic_slice_in_dim(v, r_start, wr, axis=1)
        q_r = lax.dynamic_index_in_dim(q, r, axis=1, keepdims=False)
        dr = r_start + jnp.arange(wr) - r + NA_WIN_R - 1
        bias = jnp.transpose(rpb_cols[:, dr], (0, 2, 1, 3))
        sc = jnp.einsum('bqhd,bikhd->bhqik', q_r, k_r, preferred_element_type=jnp.float32)
        sc = sc + bias[None].astype(jnp.float32)
        sc = jnp.where(col_valid[:, None, :], sc, NEG_INF)
        p = jax.nn.softmax(sc.reshape(b, NA_HEADS, GRID_W, wr * GRID_W), axis=-1)
        p = p.reshape(b, NA_HEADS, GRID_W, wr, GRID_W).astype(v.dtype)
        return jnp.einsum('bhqik,bikhd->bqhd', p, v_r)

    out = lax.map(row_block, jnp.arange(rows))
    out = jnp.transpose(out, (1, 0, 2, 3, 4)).reshape(b, s, NA_HEADS * HEAD_DIM)
    return out @ w_o


def short_conv_mixer(x, w_in, conv_w, w_out):
    gb, gc, h = jnp.split(x @ w_in, 3, axis=-1)
    return (gb * dwconv_centered(gc * h, conv_w)) @ w_out


def axial_rope_tables(s):
    t = jnp.arange(s)
    row = (t // GRID_W).astype(jnp.float32)[:, None]
    col = (t % GRID_W).astype(jnp.float32)[:, None]
    half = HEAD_DIM // 2
    inv = ROPE_THETA ** (-jnp.arange(0, half, 2, dtype=jnp.float32) / half)
    ang = jnp.concatenate([row * inv, row * inv, col * inv, col * inv], axis=-1)
    return jnp.cos(ang), jnp.sin(ang)


def rotate_half_axial(x):
    lead = x.shape[:-1]
    xs = x.reshape(*lead, 2, 2, HEAD_DIM // 4)
    x1, x2 = xs[..., 0, :], xs[..., 1, :]
    return jnp.stack([-x2, x1], axis=-2).reshape(*lead, HEAD_DIM)


def apply_axial_rope(x, cos, sin):
    xf = x.astype(jnp.float32)
    return (xf * cos[:, None, :] + rotate_half_axial(xf) * sin[:, None, :]).astype(x.dtype)


def gqa_axial_attention(x, w_qkv, q_norm, k_norm, w_o):
    b, s, _ = x.shape
    qkv = x @ w_qkv
    nq = GQA_Q_HEADS * HEAD_DIM
    nk = GQA_KV_HEADS * HEAD_DIM
    q = qkv[..., :nq].reshape(b, s, GQA_Q_HEADS, HEAD_DIM)
    k = qkv[..., nq:nq + nk].reshape(b, s, GQA_KV_HEADS, HEAD_DIM)
    v = qkv[..., nq + nk:].reshape(b, s, GQA_KV_HEADS, HEAD_DIM)
    cos, sin = axial_rope_tables(s)
    q = apply_axial_rope(rms_norm(q, q_norm), cos, sin) * (HEAD_DIM ** -0.5)
    k = apply_axial_rope(rms_norm(k, k_norm), cos, sin)
    qb = q.reshape(b, s // Q_BLOCK, Q_BLOCK, GQA_KV_HEADS, GQA_GROUP, HEAD_DIM)

    def block(q_blk):
        sc = jnp.einsum('bqkgd,bskd->bkgqs', q_blk, k, preferred_element_type=jnp.float32)
        p = jax.nn.softmax(sc, axis=-1).astype(v.dtype)
        return jnp.einsum('bkgqs,bskd->bqkgd', p, v)

    out = lax.map(block, jnp.moveaxis(qb, 1, 0))
    out = jnp.moveaxis(out, 0, 1).reshape(b, s, nq)
    return out @ w_o


def conv_glu_ffn(x, w_up, conv_w, conv_b, w_down):
    h = dwconv_centered(x @ w_up, conv_w) + conv_b
    g, u = jnp.split(h, 2, axis=-1)
    return (jax.nn.silu(g) * u) @ w_down


def setup_inputs(seed: int = 0) -> dict:
    key = jax.random.key(seed)
    ks = iter(jax.random.split(key, 32))
    n_a, n_b, n_c = (len(range(m, DEPTH, N_MIXERS)) for m in range(N_MIXERS))

    def w(shape, fan_in):
        return jax.random.normal(next(ks), shape, jnp.float32) * (fan_in ** -0.5)

    def gain(shape):
        return 1.0 + 0.02 * jax.random.normal(next(ks), shape, jnp.float32)

    gqa_cols = (GQA_Q_HEADS + 2 * GQA_KV_HEADS) * HEAD_DIM
    return {
        "x": jax.random.normal(next(ks), (BATCH, SEQ, D_MODEL), jnp.float32),
        "mix_norm": gain((DEPTH, D_MODEL)),
        "ffn_norm": gain((DEPTH, D_MODEL)),
        "final_norm": gain((D_MODEL,)),
        "na_w_qkv": w((n_a, D_MODEL, 3 * NA_HEADS * HEAD_DIM), D_MODEL),
        "na_rpb": 0.1 * jax.random.normal(next(ks), (n_a, NA_HEADS, 2 * NA_WIN_R - 1, 2 * NA_WIN_C - 1), jnp.float32),
        "na_w_o": w((n_a, NA_HEADS * HEAD_DIM, D_MODEL), NA_HEADS * HEAD_DIM),
        "sc_w_in": w((n_b, D_MODEL, 3 * D_MODEL), D_MODEL),
        "sc_conv_w": w((n_b, SC_WIDTH, D_MODEL), SC_WIDTH),
        "sc_w_out": w((n_b, D_MODEL, D_MODEL), D_MODEL),
        "gqa_w_qkv": w((n_c, D_MODEL, gqa_cols), D_MODEL),
        "gqa_q_norm": gain((n_c, HEAD_DIM)),
        "gqa_k_norm": gain((n_c, HEAD_DIM)),
        "gqa_w_o": w((n_c, GQA_Q_HEADS * HEAD_DIM, D_MODEL), GQA_Q_HEADS * HEAD_DIM),
        "ffn_w_up": w((DEPTH, D_MODEL, 2 * D_FF), D_MODEL),
        "ffn_conv_w": w((DEPTH, FFN_CONV_WIDTH, 2 * D_FF), FFN_CONV_WIDTH),
        "ffn_conv_b": 0.01 * jax.random.normal(next(ks), (DEPTH, 2 * D_FF), jnp.float32),
        "ffn_w_down": w((DEPTH, D_FF, D_MODEL), D_FF),
    }


def reference(x, mix_norm, ffn_norm, final_norm, na_w_qkv, na_rpb, na_w_o,
              sc_w_in, sc_conv_w, sc_w_out, gqa_w_qkv, gqa_q_norm, gqa_k_norm,
              gqa_w_o, ffn_w_up, ffn_conv_w, ffn_conv_b, ffn_w_down):
    h = x
    for i in range(DEPTH):
        m, j = i % N_MIXERS, i // N_MIXERS
        a = rms_norm(h, mix_norm[i])
        if m == 0:
            mixed = neighborhood_attention(a, na_w_qkv[j], na_rpb[j], na_w_o[j])
        elif m == 1:
            mixed = short_conv_mixer(a, sc_w_in[j], sc_conv_w[j], sc_w_out[j])
        else:
            mixed = gqa_axial_attention(a, gqa_w_qkv[j], gqa_q_norm[j], gqa_k_norm[j], gqa_w_o[j])
        h = h + mixed
        h = h + conv_glu_ffn(rms_norm(h, ffn_norm[i]), ffn_w_up[i], ffn_conv_w[i],
                             ffn_conv_b[i], ffn_w_down[i])
    return rms_norm(h, final_norm)
```

```python
from functools import partial

import jax
import jax.numpy as jnp
from jax import lax
from jax.experimental import pallas as pl
from jax.experimental.pallas import tpu as pltpu

D_MODEL = 2048
SEQ = 2048
GRID_W = 64
GRID_ROWS = SEQ // GRID_W
HEAD_DIM = 128
EPS = 1e-6
NEG_INF = -1e30
NA_HEADS = D_MODEL // HEAD_DIM
NA_WIN_R = 8
NA_WIN_C = 16
GQA_Q_HEADS = D_MODEL // HEAD_DIM
GQA_KV_HEADS = GQA_Q_HEADS // 4
GQA_GROUP = GQA_Q_HEADS // GQA_KV_HEADS
ROPE_THETA = 10000.0
N_MIXERS = 3

HALO = 16
VMEM_LIMIT = 48 * 1024 * 1024

BF16 = jnp.bfloat16
F32 = jnp.float32


def _params(*sem):
    return pltpu.CompilerParams(dimension_semantics=sem, vmem_limit_bytes=VMEM_LIMIT)


def _rms(x, gain):
    return x * lax.rsqrt(jnp.mean(x * x, axis=-1, keepdims=True) + EPS) * gain


def _norm_matmul_kernel(x_ref, g_ref, w_ref, o_ref, a_s):
    @pl.when(pl.program_id(1) == 0)
    def _():
        a_s[...] = _rms(x_ref[...], g_ref[...]).astype(BF16)

    o_ref[...] = jnp.dot(a_s[...], w_ref[...], preferred_element_type=F32).astype(o_ref.dtype)


def _norm_matmul(x, gain, w, *, tm=1024, tn=1024):
    t, d = x.shape
    n = w.shape[1]
    return pl.pallas_call(
        _norm_matmul_kernel,
        grid=(t // tm, n // tn),
        in_specs=[
            pl.BlockSpec((tm, d), lambda i, j: (i, 0)),
            pl.BlockSpec((1, d), lambda i, j: (0, 0)),
            pl.BlockSpec((d, tn), lambda i, j: (0, j)),
        ],
        out_specs=pl.BlockSpec((tm, tn), lambda i, j: (i, j)),
        out_shape=jax.ShapeDtypeStruct((t, n), BF16),
        scratch_shapes=[pltpu.VMEM((tm, d), BF16)],
        compiler_params=_params("parallel", "arbitrary"),
        name="norm_matmul",
    )(x, gain.reshape(1, d), w)


def _matmul_resid_kernel(a_ref, w_ref, r_ref, o_ref):
    o_ref[...] = r_ref[...] + jnp.dot(a_ref[...], w_ref[...], preferred_element_type=F32)


def _matmul_resid(a, w, resid, *, tm=1024, tn=1024):
    t, k = a.shape
    n = w.shape[1]
    return pl.pallas_call(
        _matmul_resid_kernel,
        grid=(t // tm, n // tn),
        in_specs=[
            pl.BlockSpec((tm, k), lambda i, j: (i, 0)),
            pl.BlockSpec((k, tn), lambda i, j: (0, j)),
            pl.BlockSpec((tm, tn), lambda i, j: (i, j)),
        ],
        out_specs=pl.BlockSpec((tm, tn), lambda i, j: (i, j)),
        out_shape=jax.ShapeDtypeStruct((t, n), F32),
        compiler_params=_params("parallel", "arbitrary"),
        name="matmul_resid",
    )(a, w, resid)


def _ffn_kernel(h_ref, hp_ref, hn_ref, g_ref, wg_ref, wu_ref, cwg_ref, cwu_ref,
                cbg_ref, cbu_ref, wd_ref, o_ref, a_s, *, tm, seq_tiles):
    i = pl.program_id(0)
    rows = tm + 2 * HALO

    @pl.when(pl.program_id(1) == 0)
    def _():
        gain = g_ref[...]
        a_s[HALO:HALO + tm, :] = _rms(h_ref[...], gain).astype(BF16)
        first = (i % seq_tiles) == 0
        last = (i % seq_tiles) == seq_tiles - 1
        a_s[0:HALO, :] = jnp.where(first, 0.0, _rms(hp_ref[...], gain)).astype(BF16)
        a_s[HALO + tm:rows, :] = jnp.where(last, 0.0, _rms(hn_ref[...], gain)).astype(BF16)
        o_ref[...] = h_ref[...]

    a = a_s[...]

    def conv_branch(w_ref, cw_ref, cb_ref):
        up = jnp.dot(a, w_ref[...], preferred_element_type=F32)
        cw = cw_ref[...]
        c = (pltpu.roll(up, 1, 0) * cw[0:1] + up * cw[1:2]
             + pltpu.roll(up, rows - 1, 0) * cw[2:3])
        return c[HALO:HALO + tm] + cb_ref[...]

    g = conv_branch(wg_ref, cwg_ref, cbg_ref)
    u = conv_branch(wu_ref, cwu_ref, cbu_ref)
    act = (g * jax.nn.sigmoid(g) * u).astype(BF16)
    o_ref[...] += jnp.dot(act, wd_ref[...], preferred_element_type=F32)


def _ffn(h, gain, w_up, conv_w, conv_b, w_down, *, tm=512, tn=512):
    t, d = h.shape
    d_ff = w_down.shape[0]
    nj = d_ff // tn
    seq_tiles = SEQ // tm
    hb = tm // HALO
    n_hblocks = t // HALO
    return pl.pallas_call(
        partial(_ffn_kernel, tm=tm, seq_tiles=seq_tiles),
        grid=(t // tm, nj),
        in_specs=[
            pl.BlockSpec((tm, d), lambda i, j: (i, 0)),
            pl.BlockSpec((HALO, d), lambda i, j: (jnp.maximum(i * hb - 1, 0), 0)),
            pl.BlockSpec((HALO, d), lambda i, j: (jnp.minimum((i + 1) * hb, n_hblocks - 1), 0)),
            pl.BlockSpec((1, d), lambda i, j: (0, 0)),
            pl.BlockSpec((d, tn), lambda i, j: (0, j)),
            pl.BlockSpec((d, tn), lambda i, j: (0, j + nj)),
            pl.BlockSpec((3, tn), lambda i, j: (0, j)),
            pl.BlockSpec((3, tn), lambda i, j: (0, j + nj)),
            pl.BlockSpec((1, tn), lambda i, j: (0, j)),
            pl.BlockSpec((1, tn), lambda i, j: (0, j + nj)),
            pl.BlockSpec((tn, d), lambda i, j: (j, 0)),
        ],
        out_specs=pl.BlockSpec((tm, d), lambda i, j: (i, 0)),
        out_shape=jax.ShapeDtypeStruct((t, d), F32),
        scratch_shapes=[pltpu.VMEM((tm + 2 * HALO, d), BF16)],
        compiler_params=_params("parallel", "arbitrary"),
        name="conv_glu_ffn",
    )(h, h, h, gain.reshape(1, d), w_up, w_up, conv_w, conv_w,
      conv_b.reshape(1, -1), conv_b.reshape(1, -1), w_down)


def _na_bias_table(rpb):
    cols = jnp.arange(GRID_W)
    c_start = jnp.clip(cols - NA_WIN_C // 2, 0, GRID_W - NA_WIN_C)
    col_valid = (cols[None, :] >= c_start[:, None]) & (cols[None, :] < c_start[:, None] + NA_WIN_C)
    dc = jnp.clip(cols[None, :] - cols[:, None] + NA_WIN_C - 1, 0, 2 * NA_WIN_C - 2)
    rpb_cols = rpb[:, :, dc]
    off = jnp.arange(NA_WIN_R)
    dr = jnp.arange(NA_WIN_R)[None, :] - off[:, None] + NA_WIN_R - 1
    tbl = jnp.transpose(rpb_cols[:, dr], (0, 1, 3, 2, 4))
    tbl = jnp.where(col_valid[None, None, :, None, :], tbl, NEG_INF)
    return tbl.reshape(rpb.shape[0], NA_WIN_R, GRID_W, NA_WIN_R * GRID_W).astype(F32)


def _na_kernel(q_ref, k_ref, v_ref, b_ref, o_ref):
    win = NA_WIN_R * GRID_W

    def row(r, carry):
        r_start = jnp.clip(r - NA_WIN_R // 2, 0, GRID_ROWS - NA_WIN_R)
        q0 = pl.multiple_of(r * GRID_W, GRID_W)
        k0 = pl.multiple_of(r_start * GRID_W, GRID_W)
        q = (q_ref[pl.ds(q0, GRID_W), :].astype(F32) * (HEAD_DIM ** -0.5)).astype(BF16)
        k = k_ref[pl.ds(k0, win), :]
        v = v_ref[pl.ds(k0, win), :]
        s = lax.dot_general(q, k, (((1,), (1,)), ((), ())), preferred_element_type=F32)
        s = s + b_ref[0, r - r_start]
        p = jnp.exp(s - jnp.max(s, axis=-1, keepdims=True))
        l = jnp.sum(p, axis=-1, keepdims=True)
        o = jnp.dot(p.astype(BF16), v, preferred_element_type=F32) / l
        o_ref[pl.ds(q0, GRID_W), :] = o.astype(o_ref.dtype)
        return carry

    lax.fori_loop(0, GRID_ROWS, row, 0)


def _na_attention(qkv, bias_tbl, batch):
    t = qkv.shape[0]
    blk = lambda off: pl.BlockSpec((SEQ, HEAD_DIM), lambda h, b: (b, h + off))
    return pl.pallas_call(
        _na_kernel,
        grid=(NA_HEADS, batch),
        in_specs=[
            blk(0), blk(NA_HEADS), blk(2 * NA_HEADS),
            pl.BlockSpec((1, NA_WIN_R, GRID_W, NA_WIN_R * GRID_W), lambda h, b: (h, 0, 0, 0)),
        ],
        out_specs=pl.BlockSpec((SEQ, HEAD_DIM), lambda h, b: (b, h)),
        out_shape=jax.ShapeDtypeStruct((t, NA_HEADS * HEAD_DIM), BF16),
        compiler_params=_params("parallel", "arbitrary"),
        name="neighborhood_attention",
    )(qkv, qkv, qkv, bias_tbl)


def _sc_out_kernel(gb_ref, gc_ref, hh_ref, gcp_ref, hhp_ref, gcn_ref, hhn_ref, cw_ref,
                   w_ref, r_ref, o_ref, y_s, *, tm, seq_tiles, chunk):
    i = pl.program_id(0)

    @pl.when(pl.program_id(1) == 0)
    def _():
        first = (i % seq_tiles) == 0
        last = (i % seq_tiles) == seq_tiles - 1
        rows = lax.broadcasted_iota(jnp.int32, (tm, chunk), 0)
        for c in range(0, y_s.shape[1], chunk):
            cs = slice(c, c + chunk)
            p = gc_ref[:, cs].astype(F32) * hh_ref[:, cs].astype(F32)
            prev = gcp_ref[HALO - 1:HALO, cs].astype(F32) * hhp_ref[HALO - 1:HALO, cs].astype(F32)
            nxt = gcn_ref[0:1, cs].astype(F32) * hhn_ref[0:1, cs].astype(F32)
            prev = jnp.where(first, 0.0, prev)
            nxt = jnp.where(last, 0.0, nxt)
            pm1 = jnp.where(rows == 0, prev, pltpu.roll(p, 1, 0))
            pp1 = jnp.where(rows == tm - 1, nxt, pltpu.roll(p, tm - 1, 0))
            cw = cw_ref[:, cs]
            conv = pm1 * cw[0:1] + p * cw[1:2] + pp1 * cw[2:3]
            y_s[:, cs] = (gb_ref[:, cs].astype(F32) * conv).astype(BF16)

    o_ref[...] = r_ref[...] + jnp.dot(y_s[...], w_ref[...], preferred_element_type=F32)


def _sc_out(z, conv_w, w_out, resid, *, tm=512, tn=1024, chunk=512):
    t = z.shape[0]
    d = w_out.shape[0]
    n = w_out.shape[1]
    seq_tiles = SEQ // tm
    hb = tm // HALO
    n_hblocks = t // HALO
    main = lambda c: pl.BlockSpec((tm, d), lambda i, j: (i, c))
    prev = lambda c: pl.BlockSpec((HALO, d), lambda i, j: (jnp.maximum(i * hb - 1, 0), c))
    nxt = lambda c: pl.BlockSpec((HALO, d), lambda i, j: (jnp.minimum((i + 1) * hb, n_hblocks - 1), c))
    return pl.pallas_call(
        partial(_sc_out_kernel, tm=tm, seq_tiles=seq_tiles, chunk=chunk),
        grid=(t // tm, n // tn),
        in_specs=[
            main(0), main(1), main(2), prev(1), prev(2), nxt(1), nxt(2),
            pl.BlockSpec((3, d), lambda i, j: (0, 0)),
            pl.BlockSpec((d, tn), lambda i, j: (0, j)),
            pl.BlockSpec((tm, tn), lambda i, j: (i, j)),
        ],
        out_specs=pl.BlockSpec((tm, tn), lambda i, j: (i, j)),
        out_shape=jax.ShapeDtypeStruct((t, n), F32),
        scratch_shapes=[pltpu.VMEM((tm, d), BF16)],
        compiler_params=_params("parallel", "arbitrary"),
        name="short_conv_out",
    )(z, z, z, z, z, z, z, conv_w, w_out, resid)


def _rope_tables():
    t = jnp.arange(SEQ)
    row = (t // GRID_W).astype(F32)[:, None]
    col = (t % GRID_W).astype(F32)[:, None]
    half = HEAD_DIM // 2
    inv = ROPE_THETA ** (-jnp.arange(0, half, 2, dtype=F32) / half)
    ang = jnp.concatenate([row * inv, row * inv, col * inv, col * inv], axis=-1)
    cos, sin = jnp.cos(ang), jnp.sin(ang)
    lane = jnp.arange(HEAD_DIM)[None, :]
    lower = (lane % half) < (half // 2)
    sin_lo = jnp.where(lower, -sin, 0.0)
    sin_hi = jnp.where(lower, 0.0, sin)
    return cos, sin_lo, sin_hi


def _norm_rope(x, gain, cos, sin_lo, sin_hi):
    y = _rms(x, gain)
    quarter = HEAD_DIM // 4
    return (y * cos + pltpu.roll(y, HEAD_DIM - quarter, 1) * sin_lo
            + pltpu.roll(y, quarter, 1) * sin_hi)


def _gqa_kernel(q_ref, k_ref, v_ref, qn_ref, kn_ref, cq_ref, slq_ref, shq_ref,
                ck_ref, slk_ref, shk_ref, o_ref, kt_s):
    @pl.when(pl.program_id(2) == 0)
    def _():
        k = _norm_rope(k_ref[...].astype(F32), kn_ref[...], ck_ref[...], slk_ref[...], shk_ref[...])
        kt_s[...] = k.T.astype(BF16)

    kt = kt_s[...]
    v = v_ref[...]
    cos, sin_lo, sin_hi = cq_ref[...], slq_ref[...], shq_ref[...]
    for g in range(GQA_GROUP):
        hs = slice(g * HEAD_DIM, (g + 1) * HEAD_DIM)
        q = _norm_rope(q_ref[:, hs].astype(F32), qn_ref[...], cos, sin_lo, sin_hi)
        q = (q * (HEAD_DIM ** -0.5)).astype(BF16)
        s = jnp.dot(q, kt, preferred_element_type=F32)
        p = jnp.exp(s - jnp.max(s, axis=-1, keepdims=True))
        l = jnp.sum(p, axis=-1, keepdims=True)
        o = jnp.dot(p.astype(BF16), v, preferred_element_type=F32) / l
        o_ref[:, hs] = o.astype(o_ref.dtype)


def _gqa_attention(qkv, q_norm, k_norm, batch, *, tq=256):
    t = qkv.shape[0]
    nq = SEQ // tq
    gw = GQA_GROUP * HEAD_DIM
    cos, sin_lo, sin_hi = _rope_tables()
    qtab = pl.BlockSpec((tq, HEAD_DIM), lambda b, h, i: (i, 0))
    ktab = pl.BlockSpec((SEQ, HEAD_DIM), lambda b, h, i: (0, 0))
    vec = pl.BlockSpec((1, HEAD_DIM), lambda b, h, i: (0, 0))
    return pl.pallas_call(
        _gqa_kernel,
        grid=(batch, GQA_KV_HEADS, nq),
        in_specs=[
            pl.BlockSpec((tq, gw), lambda b, h, i: (b * nq + i, h)),
            pl.BlockSpec((SEQ, HEAD_DIM), lambda b, h, i: (b, GQA_Q_HEADS + h)),
            pl.BlockSpec((SEQ, HEAD_DIM), lambda b, h, i: (b, GQA_Q_HEADS + GQA_KV_HEADS + h)),
            vec, vec, qtab, qtab, qtab, ktab, ktab, ktab,
        ],
        out_specs=pl.BlockSpec((tq, gw), lambda b, h, i: (b * nq + i, h)),
        out_shape=jax.ShapeDtypeStruct((t, GQA_Q_HEADS * HEAD_DIM), BF16),
        scratch_shapes=[pltpu.VMEM((HEAD_DIM, SEQ), BF16)],
        compiler_params=_params("parallel", "parallel", "arbitrary"),
        name="gqa_attention",
    )(qkv, qkv, qkv, q_norm.reshape(1, -1), k_norm.reshape(1, -1),
      cos, sin_lo, sin_hi, cos, sin_lo, sin_hi)


def _final_norm_kernel(x_ref, g_ref, o_ref):
    o_ref[...] = _rms(x_ref[...], g_ref[...])


def _final_norm(x, gain, *, tm=512):
    t, d = x.shape
    return pl.pallas_call(
        _final_norm_kernel,
        grid=(t // tm,),
        in_specs=[pl.BlockSpec((tm, d), lambda i: (i, 0)), pl.BlockSpec((1, d), lambda i: (0, 0))],
        out_specs=pl.BlockSpec((tm, d), lambda i: (i, 0)),
        out_shape=jax.ShapeDtypeStruct((t, d), F32),
        compiler_params=_params("parallel"),
        name="final_norm",
    )(x, gain.reshape(1, d))


def kernel(x, mix_norm, ffn_norm, final_norm, na_w_qkv, na_rpb, na_w_o, sc_w_in, sc_conv_w,
           sc_w_out, gqa_w_qkv, gqa_q_norm, gqa_k_norm, gqa_w_o, ffn_w_up, ffn_conv_w,
           ffn_conv_b, ffn_w_down):
    batch, seq, d = x.shape
    assert (seq, d) == (SEQ, D_MODEL)
    depth = mix_norm.shape[0]
    h = x.reshape(batch * seq, d)
    bf = lambda w: w.astype(BF16)
    for i in range(depth):
        m, j = i % N_MIXERS, i // N_MIXERS
        if m == 0:
            qkv = _norm_matmul(h, mix_norm[i], bf(na_w_qkv[j]))
            att = _na_attention(qkv, _na_bias_table(na_rpb[j]), batch)
            h = _matmul_resid(att, bf(na_w_o[j]), h)
        elif m == 1:
            z = _norm_matmul(h, mix_norm[i], bf(sc_w_in[j]))
            h = _sc_out(z, sc_conv_w[j], bf(sc_w_out[j]), h)
        else:
            qkv = _norm_matmul(h, mix_norm[i], bf(gqa_w_qkv[j]))
            att = _gqa_attention(qkv, gqa_q_norm[j], gqa_k_norm[j], batch)
            h = _matmul_resid(att, bf(gqa_w_o[j]), h)
        h = _ffn(h, ffn_norm[i], bf(ffn_w_up[i]), ffn_conv_w[i], ffn_conv_b[i], bf(ffn_w_down[i]))
    return _final_norm(h, final_norm).reshape(batch, seq, d)
```

```python
from functools import partial

import jax
import jax.numpy as jnp
from jax import lax
from jax.experimental import pallas as pl
from jax.experimental.pallas import tpu as pltpu

D_MODEL = 2048
SEQ = 2048
GRID_W = 64
GRID_ROWS = SEQ // GRID_W
HEAD_DIM = 128
EPS = 1e-6
NEG_INF = -1e30
NA_HEADS = D_MODEL // HEAD_DIM
NA_WIN_R = 8
NA_WIN_C = 16
GQA_Q_HEADS = D_MODEL // HEAD_DIM
GQA_KV_HEADS = GQA_Q_HEADS // 4
GQA_GROUP = GQA_Q_HEADS // GQA_KV_HEADS
ROPE_THETA = 10000.0
N_MIXERS = 3

HALO = 16
NORM_CHUNK = 64
LOG2E = 1.4426950408889634
MIB = 1024 * 1024
VMEM_LIMIT = 48 * MIB

BF16 = jnp.bfloat16
F32 = jnp.float32


def _params(*sem, vmem=VMEM_LIMIT):
    return pltpu.CompilerParams(dimension_semantics=sem, vmem_limit_bytes=vmem)


def _rms(x, gain):
    return x * lax.rsqrt(jnp.mean(x * x, axis=-1, keepdims=True) + EPS) * gain


def _norm_matmul_kernel(x_ref, g_ref, w_ref, o_ref, a_s):
    @pl.when(pl.program_id(1) == 0)
    def _():
        a_s[...] = _rms(x_ref[...], g_ref[...]).astype(BF16)

    o_ref[...] = jnp.dot(a_s[...], w_ref[...], preferred_element_type=F32).astype(o_ref.dtype)


def _norm_matmul(x, gain, w, *, tm=1024, tn=1024):
    t, d = x.shape
    n = w.shape[1]
    return pl.pallas_call(
        _norm_matmul_kernel,
        grid=(t // tm, n // tn),
        in_specs=[
            pl.BlockSpec((tm, d), lambda i, j: (i, 0)),
            pl.BlockSpec((1, d), lambda i, j: (0, 0)),
            pl.BlockSpec((d, tn), lambda i, j: (0, j)),
        ],
        out_specs=pl.BlockSpec((tm, tn), lambda i, j: (i, j)),
        out_shape=jax.ShapeDtypeStruct((t, n), BF16),
        scratch_shapes=[pltpu.VMEM((tm, d), BF16)],
        compiler_params=_params("parallel", "arbitrary"),
        name="norm_matmul",
    )(x, gain.reshape(1, d), w)


def _matmul_resid_kernel(a_ref, w_ref, r_ref, o_ref):
    o_ref[...] = r_ref[...] + jnp.dot(a_ref[...], w_ref[...], preferred_element_type=F32)


def _matmul_resid(a, w, resid, *, tm=1024, tn=1024):
    t, k = a.shape
    n = w.shape[1]
    return pl.pallas_call(
        _matmul_resid_kernel,
        grid=(t // tm, n // tn),
        in_specs=[
            pl.BlockSpec((tm, k), lambda i, j: (i, 0)),
            pl.BlockSpec((k, tn), lambda i, j: (0, j)),
            pl.BlockSpec((tm, tn), lambda i, j: (i, j)),
        ],
        out_specs=pl.BlockSpec((tm, tn), lambda i, j: (i, j)),
        out_shape=jax.ShapeDtypeStruct((t, n), F32),
        compiler_params=_params("parallel", "arbitrary"),
        name="matmul_resid",
    )(a, w, resid)


def _ffn_kernel(h_ref, hp_ref, hn_ref, g_ref, wg_ref, wu_ref, cwg_ref, cwu_ref,
                cbg_ref, cbu_ref, wd_ref, o_ref, a_s, *, tm, seq_tiles):
    i = pl.program_id(0)
    j = pl.program_id(1)
    rows = tm + 2 * HALO

    @pl.when(j == 0)
    def _():
        gain = g_ref[...]
        for c in range(0, tm, NORM_CHUNK):
            x = h_ref[c:c + NORM_CHUNK, :]
            o_ref[c:c + NORM_CHUNK, :] = x
            a_s[HALO + c:HALO + c + NORM_CHUNK, :] = _rms(x, gain).astype(BF16)
        first = (i % seq_tiles) == 0
        last = (i % seq_tiles) == seq_tiles - 1
        a_s[0:HALO, :] = jnp.where(first, 0.0, _rms(hp_ref[...], gain)).astype(BF16)
        a_s[HALO + tm:rows, :] = jnp.where(last, 0.0, _rms(hn_ref[...], gain)).astype(BF16)

    a = a_s[...]

    def conv_branch(w_ref, cw_ref, cb_ref):
        up = jnp.dot(a, w_ref[...], preferred_element_type=F32)
        cw = cw_ref[...]
        c = (pltpu.roll(up, 1, 0) * cw[0:1] + up * cw[1:2]
             + pltpu.roll(up, rows - 1, 0) * cw[2:3])
        return c[HALO:HALO + tm] + cb_ref[...]

    g = conv_branch(wg_ref, cwg_ref, cbg_ref)
    u = conv_branch(wu_ref, cwu_ref, cbu_ref)
    act = (g * jax.nn.sigmoid(g) * u).astype(BF16)
    o_ref[...] += jnp.dot(act, wd_ref[...], preferred_element_type=F32)


def _ffn(h, gain, w_up, conv_w, conv_b, w_down, *, tm=1024, tn=512):
    t, d = h.shape
    d_ff = w_down.shape[0]
    nj = d_ff // tn
    seq_tiles = SEQ // tm
    hb = tm // HALO
    n_hblocks = t // HALO
    return pl.pallas_call(
        partial(_ffn_kernel, tm=tm, seq_tiles=seq_tiles),
        grid=(t // tm, nj),
        in_specs=[
            pl.BlockSpec((tm, d), lambda i, j: (i, 0)),
            pl.BlockSpec((HALO, d), lambda i, j: (jnp.maximum(i * hb - 1, 0), 0)),
            pl.BlockSpec((HALO, d), lambda i, j: (jnp.minimum((i + 1) * hb, n_hblocks - 1), 0)),
            pl.BlockSpec((1, d), lambda i, j: (0, 0)),
            pl.BlockSpec((d, tn), lambda i, j: (0, j)),
            pl.BlockSpec((d, tn), lambda i, j: (0, j + nj)),
            pl.BlockSpec((3, tn), lambda i, j: (0, j)),
            pl.BlockSpec((3, tn), lambda i, j: (0, j + nj)),
            pl.BlockSpec((1, tn), lambda i, j: (0, j)),
            pl.BlockSpec((1, tn), lambda i, j: (0, j + nj)),
            pl.BlockSpec((tn, d), lambda i, j: (j, 0)),
        ],
        out_specs=pl.BlockSpec((tm, d), lambda i, j: (i, 0)),
        out_shape=jax.ShapeDtypeStruct((t, d), F32),
        scratch_shapes=[pltpu.VMEM((tm + 2 * HALO, d), BF16)],
        compiler_params=_params("parallel", "arbitrary", vmem=60 * MIB),
        name="conv_glu_ffn",
    )(h, h, h, gain.reshape(1, d), w_up, w_up, conv_w, conv_w,
      conv_b.reshape(1, -1), conv_b.reshape(1, -1), w_down)


def _na_bias_table(rpb):
    cols = jnp.arange(GRID_W)
    c_start = jnp.clip(cols - NA_WIN_C // 2, 0, GRID_W - NA_WIN_C)
    col_valid = (cols[None, :] >= c_start[:, None]) & (cols[None, :] < c_start[:, None] + NA_WIN_C)
    dc = jnp.clip(cols[None, :] - cols[:, None] + NA_WIN_C - 1, 0, 2 * NA_WIN_C - 2)
    rpb_cols = rpb[:, :, dc]
    off = jnp.arange(NA_WIN_R)
    dr = jnp.arange(NA_WIN_R)[None, :] - off[:, None] + NA_WIN_R - 1
    tbl = jnp.transpose(rpb_cols[:, dr], (0, 1, 3, 2, 4))
    tbl = jnp.where(col_valid[None, None, :, None, :], tbl * LOG2E, NEG_INF)
    return tbl.reshape(rpb.shape[0], NA_WIN_R, GRID_W, NA_WIN_R * GRID_W).astype(F32)


def _with_ones_column(v):
    return jnp.concatenate([v, jnp.ones_like(v)], axis=1)


def _na_kernel(q_ref, k_ref, v_ref, b_ref, o_ref, s_s, p_s, v_s, *, group):
    win = NA_WIN_R * GRID_W
    n_groups = GRID_ROWS // group
    grows = group * GRID_W
    v_s[...] = _with_ones_column(v_ref[...])

    def r_start(r):
        return min(max(r - NA_WIN_R // 2, 0), GRID_ROWS - NA_WIN_R)

    def scores(g):
        for r in range(g * group, (g + 1) * group):
            q0, k0 = r * GRID_W, r_start(r) * GRID_W
            q = (q_ref[q0:q0 + GRID_W, :].astype(F32) * (HEAD_DIM ** -0.5 * LOG2E)).astype(BF16)
            s = lax.dot_general(q, k_ref[k0:k0 + win, :], (((1,), (1,)), ((), ())),
                                preferred_element_type=F32)
            s_s[q0:q0 + GRID_W, :] = s + b_ref[0, r - r_start(r)]

    def softmax(g):
        rs = slice(g * grows, (g + 1) * grows)
        s = s_s[rs, :]
        p_s[rs, :] = jnp.exp2(s - jnp.max(s, axis=-1, keepdims=True)).astype(BF16)

    def weighted_values(g):
        for r in range(g * group, (g + 1) * group):
            q0, k0 = r * GRID_W, r_start(r) * GRID_W
            o = jnp.dot(p_s[q0:q0 + GRID_W, :], v_s[k0:k0 + win, :], preferred_element_type=F32)
            o_ref[q0:q0 + GRID_W, :] = (o[:, :HEAD_DIM] / o[:, HEAD_DIM:HEAD_DIM + 1]).astype(o_ref.dtype)

    scores(0)
    for g in range(n_groups):
        if g + 1 < n_groups:
            scores(g + 1)
        softmax(g)
        weighted_values(g)


def _na_attention(qkv, bias_tbl, batch, *, group=4):
    t = qkv.shape[0]
    win = NA_WIN_R * GRID_W
    blk = lambda off: pl.BlockSpec((SEQ, HEAD_DIM), lambda h, b: (b, h + off))
    return pl.pallas_call(
        partial(_na_kernel, group=group),
        grid=(NA_HEADS, batch),
        in_specs=[
            blk(0), blk(NA_HEADS), blk(2 * NA_HEADS),
            pl.BlockSpec((1, NA_WIN_R, GRID_W, NA_WIN_R * GRID_W), lambda h, b: (h, 0, 0, 0)),
        ],
        out_specs=pl.BlockSpec((SEQ, HEAD_DIM), lambda h, b: (b, h)),
        out_shape=jax.ShapeDtypeStruct((t, NA_HEADS * HEAD_DIM), BF16),
        scratch_shapes=[pltpu.VMEM((SEQ, win), F32), pltpu.VMEM((SEQ, win), BF16),
                        pltpu.VMEM((SEQ, 2 * HEAD_DIM), BF16)],
        compiler_params=_params("parallel", "arbitrary"),
        name="neighborhood_attention",
    )(qkv, qkv, qkv, bias_tbl)


def _sc_out_kernel(gb_ref, gc_ref, hh_ref, gcp_ref, hhp_ref, gcn_ref, hhn_ref, cw_ref,
                   w_ref, r_ref, o_ref, *, tm, seq_tiles, chunk):
    i = pl.program_id(0)
    first = (i % seq_tiles) == 0
    last = (i % seq_tiles) == seq_tiles - 1
    rows = lax.broadcasted_iota(jnp.int32, (tm, chunk), 0)
    acc = r_ref[...]
    for c in range(0, w_ref.shape[0], chunk):
        cs = slice(c, c + chunk)
        p = gc_ref[:, cs].astype(F32) * hh_ref[:, cs].astype(F32)
        prev = gcp_ref[HALO - 1:HALO, cs].astype(F32) * hhp_ref[HALO - 1:HALO, cs].astype(F32)
        nxt = gcn_ref[0:1, cs].astype(F32) * hhn_ref[0:1, cs].astype(F32)
        prev = jnp.where(first, 0.0, prev)
        nxt = jnp.where(last, 0.0, nxt)
        pm1 = jnp.where(rows == 0, prev, pltpu.roll(p, 1, 0))
        pp1 = jnp.where(rows == tm - 1, nxt, pltpu.roll(p, tm - 1, 0))
        cw = cw_ref[:, cs]
        conv = pm1 * cw[0:1] + p * cw[1:2] + pp1 * cw[2:3]
        y = (gb_ref[:, cs].astype(F32) * conv).astype(BF16)
        acc = acc + jnp.dot(y, w_ref[cs, :], preferred_element_type=F32)
    o_ref[...] = acc


def _sc_out(z, conv_w, w_out, resid, *, tm=512, chunk=512):
    t = z.shape[0]
    d, n = w_out.shape
    seq_tiles = SEQ // tm
    hb = tm // HALO
    n_hblocks = t // HALO
    main = lambda c: pl.BlockSpec((tm, d), lambda i: (i, c))
    prev = lambda c: pl.BlockSpec((HALO, d), lambda i: (jnp.maximum(i * hb - 1, 0), c))
    nxt = lambda c: pl.BlockSpec((HALO, d), lambda i: (jnp.minimum((i + 1) * hb, n_hblocks - 1), c))
    return pl.pallas_call(
        partial(_sc_out_kernel, tm=tm, seq_tiles=seq_tiles, chunk=chunk),
        grid=(t // tm,),
        in_specs=[
            main(0), main(1), main(2), prev(1), prev(2), nxt(1), nxt(2),
            pl.BlockSpec((3, d), lambda i: (0, 0)),
            pl.BlockSpec((d, n), lambda i: (0, 0)),
            pl.BlockSpec((tm, n), lambda i: (i, 0)),
        ],
        out_specs=pl.BlockSpec((tm, n), lambda i: (i, 0)),
        out_shape=jax.ShapeDtypeStruct((t, n), F32),
        compiler_params=_params("parallel"),
        name="short_conv_out",
    )(z, z, z, z, z, z, z, conv_w, w_out, resid)


def _rope_tables():
    t = jnp.arange(SEQ)
    row = (t // GRID_W).astype(F32)[:, None]
    col = (t % GRID_W).astype(F32)[:, None]
    half = HEAD_DIM // 2
    inv = ROPE_THETA ** (-jnp.arange(0, half, 2, dtype=F32) / half)
    ang = jnp.concatenate([row * inv, row * inv, col * inv, col * inv], axis=-1)
    cos, sin = jnp.cos(ang), jnp.sin(ang)
    lane = jnp.arange(HEAD_DIM)[None, :]
    lower = (lane % half) < (half // 2)
    sin_lo = jnp.where(lower, -sin, 0.0)
    sin_hi = jnp.where(lower, 0.0, sin)
    return cos, sin_lo, sin_hi


def _norm_rope(x, gain, cos, sin_lo, sin_hi):
    y = _rms(x, gain)
    quarter = HEAD_DIM // 4
    return (y * cos + pltpu.roll(y, HEAD_DIM - quarter, 1) * sin_lo
            + pltpu.roll(y, quarter, 1) * sin_hi)


def _gqa_kernel(q_ref, k_ref, v_ref, qn_ref, kn_ref, cq_ref, slq_ref, shq_ref,
                ck_ref, slk_ref, shk_ref, o_ref, kt_s, v_s):
    @pl.when(pl.program_id(2) == 0)
    def _():
        k = _norm_rope(k_ref[...].astype(F32), kn_ref[...], ck_ref[...], slk_ref[...], shk_ref[...])
        kt_s[...] = k.T.astype(BF16)
        v_s[...] = _with_ones_column(v_ref[...])

    kt = kt_s[...]
    v = v_s[...]
    cos, sin_lo, sin_hi = cq_ref[...], slq_ref[...], shq_ref[...]
    for g in range(GQA_GROUP):
        hs = slice(g * HEAD_DIM, (g + 1) * HEAD_DIM)
        q = _norm_rope(q_ref[:, hs].astype(F32), qn_ref[...], cos, sin_lo, sin_hi)
        q = (q * (HEAD_DIM ** -0.5 * LOG2E)).astype(BF16)
        s = jnp.dot(q, kt, preferred_element_type=F32)
        p = jnp.exp2(s - jnp.max(s, axis=-1, keepdims=True)).astype(BF16)
        o = jnp.dot(p, v, preferred_element_type=F32)
        o_ref[:, hs] = (o[:, :HEAD_DIM] / o[:, HEAD_DIM:HEAD_DIM + 1]).astype(o_ref.dtype)


def _gqa_attention(qkv, q_norm, k_norm, batch, *, tq=256):
    t = qkv.shape[0]
    nq = SEQ // tq
    gw = GQA_GROUP * HEAD_DIM
    cos, sin_lo, sin_hi = _rope_tables()
    qtab = pl.BlockSpec((tq, HEAD_DIM), lambda b, h, i: (i, 0))
    ktab = pl.BlockSpec((SEQ, HEAD_DIM), lambda b, h, i: (0, 0))
    vec = pl.BlockSpec((1, HEAD_DIM), lambda b, h, i: (0, 0))
    return pl.pallas_call(
        _gqa_kernel,
        grid=(batch, GQA_KV_HEADS, nq),
        in_specs=[
            pl.BlockSpec((tq, gw), lambda b, h, i: (b * nq + i, h)),
            pl.BlockSpec((SEQ, HEAD_DIM), lambda b, h, i: (b, GQA_Q_HEADS + h)),
            pl.BlockSpec((SEQ, HEAD_DIM), lambda b, h, i: (b, GQA_Q_HEADS + GQA_KV_HEADS + h)),
            vec, vec, qtab, qtab, qtab, ktab, ktab, ktab,
        ],
        out_specs=pl.BlockSpec((tq, gw), lambda b, h, i: (b * nq + i, h)),
        out_shape=jax.ShapeDtypeStruct((t, GQA_Q_HEADS * HEAD_DIM), BF16),
        scratch_shapes=[pltpu.VMEM((HEAD_DIM, SEQ), BF16), pltpu.VMEM((SEQ, 2 * HEAD_DIM), BF16)],
        compiler_params=_params("parallel", "parallel", "arbitrary"),
        name="gqa_attention",
    )(qkv, qkv, qkv, q_norm.reshape(1, -1), k_norm.reshape(1, -1),
      cos, sin_lo, sin_hi, cos, sin_lo, sin_hi)


def _final_norm_kernel(x_ref, g_ref, o_ref):
    o_ref[...] = _rms(x_ref[...], g_ref[...])


def _final_norm(x, gain, *, tm=512):
    t, d = x.shape
    return pl.pallas_call(
        _final_norm_kernel,
        grid=(t // tm,),
        in_specs=[pl.BlockSpec((tm, d), lambda i: (i, 0)), pl.BlockSpec((1, d), lambda i: (0, 0))],
        out_specs=pl.BlockSpec((tm, d), lambda i: (i, 0)),
        out_shape=jax.ShapeDtypeStruct((t, d), F32),
        compiler_params=_params("parallel"),
        name="final_norm",
    )(x, gain.reshape(1, d))


def kernel(x, mix_norm, ffn_norm, final_norm, na_w_qkv, na_rpb, na_w_o, sc_w_in, sc_conv_w,
           sc_w_out, gqa_w_qkv, gqa_q_norm, gqa_k_norm, gqa_w_o, ffn_w_up, ffn_conv_w,
           ffn_conv_b, ffn_w_down):
    batch, seq, d = x.shape
    assert (seq, d) == (SEQ, D_MODEL)
    depth = mix_norm.shape[0]
    h = x.reshape(batch * seq, d)
    bf = lambda w: w.astype(BF16)
    for i in range(depth):
        m, j = i % N_MIXERS, i // N_MIXERS
        if m == 0:
            qkv = _norm_matmul(h, mix_norm[i], bf(na_w_qkv[j]))
            att = _na_attention(qkv, _na_bias_table(na_rpb[j]), batch)
            h = _matmul_resid(att, bf(na_w_o[j]), h)
        elif m == 1:
            z = _norm_matmul(h, mix_norm[i], bf(sc_w_in[j]))
            h = _sc_out(z, sc_conv_w[j], bf(sc_w_out[j]), h)
        else:
            qkv = _norm_matmul(h, mix_norm[i], bf(gqa_w_qkv[j]))
            att = _gqa_attention(qkv, gqa_q_norm[j], gqa_k_norm[j], batch)
            h = _matmul_resid(att, bf(gqa_w_o[j]), h)
        h = _ffn(h, ffn_norm[i], bf(ffn_w_up[i]), ffn_conv_w[i], ffn_conv_b[i], bf(ffn_w_down[i]))
    return _final_norm(h, final_norm).reshape(batch, seq, d)
```

```python
from functools import partial

import jax
import jax.numpy as jnp
from jax import lax
from jax.experimental import pallas as pl
from jax.experimental.pallas import tpu as pltpu

D_MODEL = 2048
SEQ = 2048
GRID_W = 64
GRID_ROWS = SEQ // GRID_W
HEAD_DIM = 128
EPS = 1e-6
NEG_INF = -1e30
NA_HEADS = D_MODEL // HEAD_DIM
NA_WIN_R = 8
NA_WIN_C = 16
GQA_Q_HEADS = D_MODEL // HEAD_DIM
GQA_KV_HEADS = GQA_Q_HEADS // 4
GQA_GROUP = GQA_Q_HEADS // GQA_KV_HEADS
ROPE_THETA = 10000.0
N_MIXERS = 3

HALO = 16
NORM_CHUNK = 64
LOG2E = 1.4426950408889634
MIB = 1024 * 1024
VMEM_LIMIT = 48 * MIB

BF16 = jnp.bfloat16
F32 = jnp.float32


def _params(*sem, vmem=VMEM_LIMIT):
    return pltpu.CompilerParams(dimension_semantics=sem, vmem_limit_bytes=vmem)


def _rms(x, gain):
    return x * lax.rsqrt(jnp.mean(x * x, axis=-1, keepdims=True) + EPS) * gain


def _norm_matmul_kernel(x_ref, g_ref, w_ref, o_ref, a_s):
    @pl.when(pl.program_id(1) == 0)
    def _():
        a_s[...] = _rms(x_ref[...], g_ref[...]).astype(BF16)

    o_ref[...] = jnp.dot(a_s[...], w_ref[...], preferred_element_type=F32).astype(o_ref.dtype)


def _norm_matmul(x, gain, w, layer, *, tm=1024, tn=1024):
    t, d = x.shape
    n = w.shape[2]
    return pl.pallas_call(
        _norm_matmul_kernel,
        grid=(t // tm, n // tn),
        in_specs=[
            pl.BlockSpec((tm, d), lambda i, j: (i, 0)),
            pl.BlockSpec((1, d), lambda i, j: (0, 0)),
            pl.BlockSpec((None, d, tn), lambda i, j: (layer, 0, j)),
        ],
        out_specs=pl.BlockSpec((tm, tn), lambda i, j: (i, j)),
        out_shape=jax.ShapeDtypeStruct((t, n), BF16),
        scratch_shapes=[pltpu.VMEM((tm, d), BF16)],
        compiler_params=_params("parallel", "arbitrary"),
        name="norm_matmul",
    )(x, gain.reshape(1, d), w)


def _matmul_resid_kernel(a_ref, w_ref, r_ref, o_ref):
    o_ref[...] = r_ref[...] + jnp.dot(a_ref[...], w_ref[...], preferred_element_type=F32)


def _matmul_resid(a, w, layer, resid, *, tm=1024, tn=1024):
    t, k = a.shape
    n = w.shape[2]
    return pl.pallas_call(
        _matmul_resid_kernel,
        grid=(t // tm, n // tn),
        in_specs=[
            pl.BlockSpec((tm, k), lambda i, j: (i, 0)),
            pl.BlockSpec((None, k, tn), lambda i, j: (layer, 0, j)),
            pl.BlockSpec((tm, tn), lambda i, j: (i, j)),
        ],
        out_specs=pl.BlockSpec((tm, tn), lambda i, j: (i, j)),
        out_shape=jax.ShapeDtypeStruct((t, n), F32),
        compiler_params=_params("parallel", "arbitrary"),
        name="matmul_resid",
    )(a, w, resid)


def _ffn_kernel(h_ref, hp_ref, hn_ref, g_ref, wg_ref, wu_ref, cwg_ref, cwu_ref,
                cbg_ref, cbu_ref, wd_ref, fg_ref, o_ref, a_s, *, tm, seq_tiles, final_norm):
    i = pl.program_id(0)
    j = pl.program_id(1)
    rows = tm + 2 * HALO

    @pl.when(j == 0)
    def _():
        gain = g_ref[...]
        for c in range(0, tm, NORM_CHUNK):
            x = h_ref[c:c + NORM_CHUNK, :]
            o_ref[c:c + NORM_CHUNK, :] = x
            a_s[HALO + c:HALO + c + NORM_CHUNK, :] = _rms(x, gain).astype(BF16)
        first = (i % seq_tiles) == 0
        last = (i % seq_tiles) == seq_tiles - 1
        a_s[0:HALO, :] = jnp.where(first, 0.0, _rms(hp_ref[...], gain)).astype(BF16)
        a_s[HALO + tm:rows, :] = jnp.where(last, 0.0, _rms(hn_ref[...], gain)).astype(BF16)

    a = a_s[...]

    def conv_branch(w_ref, cw_ref, cb_ref):
        up = jnp.dot(a, w_ref[...], preferred_element_type=F32)
        cw = cw_ref[...]
        c = (pltpu.roll(up, 1, 0) * cw[0:1] + up * cw[1:2]
             + pltpu.roll(up, rows - 1, 0) * cw[2:3])
        return c[HALO:HALO + tm] + cb_ref[...]

    g = conv_branch(wg_ref, cwg_ref, cbg_ref)
    u = conv_branch(wu_ref, cwu_ref, cbu_ref)
    act = (g * jax.nn.sigmoid(g) * u).astype(BF16)
    o_ref[...] += jnp.dot(act, wd_ref[...], preferred_element_type=F32)

    if final_norm:
        @pl.when(j == pl.num_programs(1) - 1)
        def _():
            for c in range(0, tm, NORM_CHUNK):
                o_ref[c:c + NORM_CHUNK, :] = _rms(o_ref[c:c + NORM_CHUNK, :], fg_ref[...])


def _ffn(h, gain, w_up, conv_w, conv_b, w_down, layer, final_gain, *, final_norm, tm=1024, tn=512):
    t, d = h.shape
    d_ff = w_down.shape[1]
    nj = d_ff // tn
    seq_tiles = SEQ // tm
    hb = tm // HALO
    n_hblocks = t // HALO
    return pl.pallas_call(
        partial(_ffn_kernel, tm=tm, seq_tiles=seq_tiles, final_norm=final_norm),
        grid=(t // tm, nj),
        in_specs=[
            pl.BlockSpec((tm, d), lambda i, j: (i, 0)),
            pl.BlockSpec((HALO, d), lambda i, j: (jnp.maximum(i * hb - 1, 0), 0)),
            pl.BlockSpec((HALO, d), lambda i, j: (jnp.minimum((i + 1) * hb, n_hblocks - 1), 0)),
            pl.BlockSpec((1, d), lambda i, j: (0, 0)),
            pl.BlockSpec((None, d, tn), lambda i, j: (layer, 0, j)),
            pl.BlockSpec((None, d, tn), lambda i, j: (layer, 0, j + nj)),
            pl.BlockSpec((None, 3, tn), lambda i, j: (layer, 0, j)),
            pl.BlockSpec((None, 3, tn), lambda i, j: (layer, 0, j + nj)),
            pl.BlockSpec((None, 1, tn), lambda i, j: (layer, 0, j)),
            pl.BlockSpec((None, 1, tn), lambda i, j: (layer, 0, j + nj)),
            pl.BlockSpec((None, tn, d), lambda i, j: (layer, j, 0)),
            pl.BlockSpec((1, d), lambda i, j: (0, 0)),
        ],
        out_specs=pl.BlockSpec((tm, d), lambda i, j: (i, 0)),
        out_shape=jax.ShapeDtypeStruct((t, d), F32),
        scratch_shapes=[pltpu.VMEM((tm + 2 * HALO, d), BF16)],
        compiler_params=_params("parallel", "arbitrary", vmem=60 * MIB),
        name="conv_glu_ffn",
    )(h, h, h, gain.reshape(1, d), w_up, w_up, conv_w, conv_w, conv_b, conv_b, w_down,
      final_gain.reshape(1, d))


def _na_bias_table(rpb):
    heads = rpb.shape[0]
    cols = jnp.arange(GRID_W)
    c_start = jnp.clip(cols - NA_WIN_C // 2, 0, GRID_W - NA_WIN_C)
    col_valid = (cols[None, :] >= c_start[:, None]) & (cols[None, :] < c_start[:, None] + NA_WIN_C)
    dc = jnp.clip(cols[None, :] - cols[:, None] + NA_WIN_C - 1, 0, 2 * NA_WIN_C - 2)
    tbl = jnp.where(col_valid[None, None], rpb[:, :, dc] * LOG2E, NEG_INF)
    tbl = jnp.transpose(tbl, (0, 2, 1, 3)).reshape(heads, GRID_W, -1)
    width = 2 * NA_WIN_R * GRID_W
    even = jnp.pad(tbl, ((0, 0), (0, 0), (0, width - tbl.shape[-1])))
    odd = jnp.pad(tbl[:, :, GRID_W:], ((0, 0), (0, 0), (0, width - tbl.shape[-1] + GRID_W)))
    return jnp.stack([even, odd], axis=1).astype(F32)


def _with_ones_column(v):
    return jnp.concatenate([v, jnp.ones_like(v)], axis=1)


def _na_kernel(q_ref, k_ref, v_ref, b_ref, o_ref, s_s, p_s, v_s, *, group):
    win = NA_WIN_R * GRID_W
    n_groups = GRID_ROWS // group
    grows = group * GRID_W
    v_s[...] = _with_ones_column(v_ref[...])

    def r_start(r):
        return min(max(r - NA_WIN_R // 2, 0), GRID_ROWS - NA_WIN_R)

    def bias(r):
        first_offset = NA_WIN_R - 1 - (r - r_start(r))
        lane0 = (first_offset // 2) * 2 * GRID_W
        return b_ref[0, first_offset % 2, :, lane0:lane0 + win]

    def scores(g):
        for r in range(g * group, (g + 1) * group):
            q0, k0 = r * GRID_W, r_start(r) * GRID_W
            q = (q_ref[q0:q0 + GRID_W, :].astype(F32) * (HEAD_DIM ** -0.5 * LOG2E)).astype(BF16)
            s = lax.dot_general(q, k_ref[k0:k0 + win, :], (((1,), (1,)), ((), ())),
                                preferred_element_type=F32)
            s_s[q0:q0 + GRID_W, :] = s + bias(r)

    def softmax(g):
        rs = slice(g * grows, (g + 1) * grows)
        s = s_s[rs, :]
        p_s[rs, :] = jnp.exp2(s - jnp.max(s, axis=-1, keepdims=True)).astype(BF16)

    def weighted_values(g):
        for r in range(g * group, (g + 1) * group):
            q0, k0 = r * GRID_W, r_start(r) * GRID_W
            o = jnp.dot(p_s[q0:q0 + GRID_W, :], v_s[k0:k0 + win, :], preferred_element_type=F32)
            o_ref[q0:q0 + GRID_W, :] = (o[:, :HEAD_DIM] / o[:, HEAD_DIM:HEAD_DIM + 1]).astype(o_ref.dtype)

    scores(0)
    for g in range(n_groups):
        if g + 1 < n_groups:
            scores(g + 1)
        softmax(g)
        weighted_values(g)


def _na_attention(qkv, bias_tbl, batch, *, group=4):
    t = qkv.shape[0]
    win = NA_WIN_R * GRID_W
    blk = lambda off: pl.BlockSpec((SEQ, HEAD_DIM), lambda h, b: (b, h + off))
    return pl.pallas_call(
        partial(_na_kernel, group=group),
        grid=(NA_HEADS, batch),
        in_specs=[
            blk(0), blk(NA_HEADS), blk(2 * NA_HEADS),
            pl.BlockSpec((1, 2, GRID_W, 2 * win), lambda h, b: (h, 0, 0, 0)),
        ],
        out_specs=pl.BlockSpec((SEQ, HEAD_DIM), lambda h, b: (b, h)),
        out_shape=jax.ShapeDtypeStruct((t, NA_HEADS * HEAD_DIM), BF16),
        scratch_shapes=[pltpu.VMEM((SEQ, win), F32), pltpu.VMEM((SEQ, win), BF16),
                        pltpu.VMEM((SEQ, 2 * HEAD_DIM), BF16)],
        compiler_params=_params("parallel", "arbitrary"),
        name="neighborhood_attention",
    )(qkv, qkv, qkv, bias_tbl)


def _sc_out_kernel(gb_ref, gc_ref, hh_ref, gcp_ref, hhp_ref, gcn_ref, hhn_ref, cw_ref,
                   w_ref, r_ref, o_ref, *, tm, seq_tiles, chunk):
    i = pl.program_id(0)
    first = (i % seq_tiles) == 0
    last = (i % seq_tiles) == seq_tiles - 1
    rows = lax.broadcasted_iota(jnp.int32, (tm, chunk), 0)
    acc = r_ref[...]
    for c in range(0, w_ref.shape[0], chunk):
        cs = slice(c, c + chunk)
        p = gc_ref[:, cs].astype(F32) * hh_ref[:, cs].astype(F32)
        prev = gcp_ref[HALO - 1:HALO, cs].astype(F32) * hhp_ref[HALO - 1:HALO, cs].astype(F32)
        nxt = gcn_ref[0:1, cs].astype(F32) * hhn_ref[0:1, cs].astype(F32)
        prev = jnp.where(first, 0.0, prev)
        nxt = jnp.where(last, 0.0, nxt)
        pm1 = jnp.where(rows == 0, prev, pltpu.roll(p, 1, 0))
        pp1 = jnp.where(rows == tm - 1, nxt, pltpu.roll(p, tm - 1, 0))
        cw = cw_ref[:, cs]
        conv = pm1 * cw[0:1] + p * cw[1:2] + pp1 * cw[2:3]
        y = (gb_ref[:, cs].astype(F32) * conv).astype(BF16)
        acc = acc + jnp.dot(y, w_ref[cs, :], preferred_element_type=F32)
    o_ref[...] = acc


def _sc_out(z, conv_w, w_out, layer, resid, *, tm=512, chunk=512):
    t = z.shape[0]
    _, d, n = w_out.shape
    seq_tiles = SEQ // tm
    hb = tm // HALO
    n_hblocks = t // HALO
    main = lambda c: pl.BlockSpec((tm, d), lambda i: (i, c))
    prev = lambda c: pl.BlockSpec((HALO, d), lambda i: (jnp.maximum(i * hb - 1, 0), c))
    nxt = lambda c: pl.BlockSpec((HALO, d), lambda i: (jnp.minimum((i + 1) * hb, n_hblocks - 1), c))
    return pl.pallas_call(
        partial(_sc_out_kernel, tm=tm, seq_tiles=seq_tiles, chunk=chunk),
        grid=(t // tm,),
        in_specs=[
            main(0), main(1), main(2), prev(1), prev(2), nxt(1), nxt(2),
            pl.BlockSpec((None, 3, d), lambda i: (layer, 0, 0)),
            pl.BlockSpec((None, d, n), lambda i: (layer, 0, 0)),
            pl.BlockSpec((tm, n), lambda i: (i, 0)),
        ],
        out_specs=pl.BlockSpec((tm, n), lambda i: (i, 0)),
        out_shape=jax.ShapeDtypeStruct((t, n), F32),
        compiler_params=_params("parallel"),
        name="short_conv_out",
    )(z, z, z, z, z, z, z, conv_w, w_out, resid)


def _rope_tables():
    t = jnp.arange(SEQ)
    row = (t // GRID_W).astype(F32)[:, None]
    col = (t % GRID_W).astype(F32)[:, None]
    half = HEAD_DIM // 2
    inv = ROPE_THETA ** (-jnp.arange(0, half, 2, dtype=F32) / half)
    ang = jnp.concatenate([row * inv, row * inv, col * inv, col * inv], axis=-1)
    cos, sin = jnp.cos(ang), jnp.sin(ang)
    lane = jnp.arange(HEAD_DIM)[None, :]
    lower = (lane % half) < (half // 2)
    sin_lo = jnp.where(lower, -sin, 0.0)
    sin_hi = jnp.where(lower, 0.0, sin)
    return cos, sin_lo, sin_hi


def _norm_rope(x, gain, cos, sin_lo, sin_hi):
    y = _rms(x, gain)
    quarter = HEAD_DIM // 4
    return (y * cos + pltpu.roll(y, HEAD_DIM - quarter, 1) * sin_lo
            + pltpu.roll(y, quarter, 1) * sin_hi)


def _gqa_kernel(q_ref, k_ref, v_ref, qn_ref, kn_ref, cq_ref, slq_ref, shq_ref,
                ck_ref, slk_ref, shk_ref, o_ref, kt_s, v_s):
    @pl.when(pl.program_id(2) == 0)
    def _():
        k = _norm_rope(k_ref[...].astype(F32), kn_ref[...], ck_ref[...], slk_ref[...], shk_ref[...])
        kt_s[...] = k.T.astype(BF16)
        v_s[...] = _with_ones_column(v_ref[...])

    kt = kt_s[...]
    v = v_s[...]
    cos, sin_lo, sin_hi = cq_ref[...], slq_ref[...], shq_ref[...]
    for g in range(GQA_GROUP):
        hs = slice(g * HEAD_DIM, (g + 1) * HEAD_DIM)
        q = _norm_rope(q_ref[:, hs].astype(F32), qn_ref[...], cos, sin_lo, sin_hi)
        q = (q * (HEAD_DIM ** -0.5 * LOG2E)).astype(BF16)
        s = jnp.dot(q, kt, preferred_element_type=F32)
        p = jnp.exp2(s - jnp.max(s, axis=-1, keepdims=True)).astype(BF16)
        o = jnp.dot(p, v, preferred_element_type=F32)
        o_ref[:, hs] = (o[:, :HEAD_DIM] / o[:, HEAD_DIM:HEAD_DIM + 1]).astype(o_ref.dtype)


def _gqa_attention(qkv, q_norm, k_norm, batch, *, tq=256):
    t = qkv.shape[0]
    nq = SEQ // tq
    gw = GQA_GROUP * HEAD_DIM
    cos, sin_lo, sin_hi = _rope_tables()
    qtab = pl.BlockSpec((tq, HEAD_DIM), lambda b, h, i: (i, 0))
    ktab = pl.BlockSpec((SEQ, HEAD_DIM), lambda b, h, i: (0, 0))
    vec = pl.BlockSpec((1, HEAD_DIM), lambda b, h, i: (0, 0))
    return pl.pallas_call(
        _gqa_kernel,
        grid=(batch, GQA_KV_HEADS, nq),
        in_specs=[
            pl.BlockSpec((tq, gw), lambda b, h, i: (b * nq + i, h)),
            pl.BlockSpec((SEQ, HEAD_DIM), lambda b, h, i: (b, GQA_Q_HEADS + h)),
            pl.BlockSpec((SEQ, HEAD_DIM), lambda b, h, i: (b, GQA_Q_HEADS + GQA_KV_HEADS + h)),
            vec, vec, qtab, qtab, qtab, ktab, ktab, ktab,
        ],
        out_specs=pl.BlockSpec((tq, gw), lambda b, h, i: (b * nq + i, h)),
        out_shape=jax.ShapeDtypeStruct((t, GQA_Q_HEADS * HEAD_DIM), BF16),
        scratch_shapes=[pltpu.VMEM((HEAD_DIM, SEQ), BF16), pltpu.VMEM((SEQ, 2 * HEAD_DIM), BF16)],
        compiler_params=_params("parallel", "parallel", "arbitrary"),
        name="gqa_attention",
    )(qkv, qkv, qkv, q_norm.reshape(1, -1), k_norm.reshape(1, -1),
      cos, sin_lo, sin_hi, cos, sin_lo, sin_hi)


def kernel(x, mix_norm, ffn_norm, final_norm, na_w_qkv, na_rpb, na_w_o, sc_w_in, sc_conv_w,
           sc_w_out, gqa_w_qkv, gqa_q_norm, gqa_k_norm, gqa_w_o, ffn_w_up, ffn_conv_w,
           ffn_conv_b, ffn_w_down):
    batch, seq, d = x.shape
    assert (seq, d) == (SEQ, D_MODEL)
    depth = mix_norm.shape[0]
    h = x.reshape(batch * seq, d)
    bf = lambda w: w.astype(BF16)
    na_w_qkv, na_w_o, sc_w_in, sc_w_out = bf(na_w_qkv), bf(na_w_o), bf(sc_w_in), bf(sc_w_out)
    gqa_w_qkv, gqa_w_o, ffn_w_up, ffn_w_down = bf(gqa_w_qkv), bf(gqa_w_o), bf(ffn_w_up), bf(ffn_w_down)
    ffn_conv_b = ffn_conv_b.reshape(depth, 1, -1)
    for i in range(depth):
        m, j = i % N_MIXERS, i // N_MIXERS
        if m == 0:
            qkv = _norm_matmul(h, mix_norm[i], na_w_qkv, j)
            att = _na_attention(qkv, _na_bias_table(na_rpb[j]), batch)
            h = _matmul_resid(att, na_w_o, j, h)
        elif m == 1:
            z = _norm_matmul(h, mix_norm[i], sc_w_in, j)
            h = _sc_out(z, sc_conv_w, sc_w_out, j, h)
        else:
            qkv = _norm_matmul(h, mix_norm[i], gqa_w_qkv, j)
            att = _gqa_attention(qkv, gqa_q_norm[j], gqa_k_norm[j], batch)
            h = _matmul_resid(att, gqa_w_o, j, h)
        h = _ffn(h, ffn_norm[i], ffn_w_up, ffn_conv_w, ffn_conv_b, ffn_w_down, i, final_norm,
                 final_norm=(i == depth - 1))
    return h.reshape(batch, seq, d)
```

```python
from functools import partial

import jax
import jax.numpy as jnp
from jax import lax
from jax.experimental import pallas as pl
from jax.experimental.pallas import tpu as pltpu

D_MODEL = 2048
SEQ = 2048
GRID_W = 64
GRID_ROWS = SEQ // GRID_W
HEAD_DIM = 128
EPS = 1e-6
NEG_INF = -1e30
NA_HEADS = D_MODEL // HEAD_DIM
NA_WIN_R = 8
NA_WIN_C = 16
GQA_Q_HEADS = D_MODEL // HEAD_DIM
GQA_KV_HEADS = GQA_Q_HEADS // 4
GQA_GROUP = GQA_Q_HEADS // GQA_KV_HEADS
ROPE_THETA = 10000.0
N_MIXERS = 3

HALO = 16
NORM_CHUNK = 64
LOG2E = 1.4426950408889634
MIB = 1024 * 1024
VMEM_LIMIT = 48 * MIB

BF16 = jnp.bfloat16
F32 = jnp.float32


def _params(*sem, vmem=VMEM_LIMIT):
    return pltpu.CompilerParams(dimension_semantics=sem, vmem_limit_bytes=vmem)


def _rms(x, gain):
    return x * lax.rsqrt(jnp.mean(x * x, axis=-1, keepdims=True) + EPS) * gain


def _norm_matmul_kernel(x_ref, g_ref, w_ref, o_ref, a_s):
    @pl.when(pl.program_id(1) == 0)
    def _():
        a_s[...] = _rms(x_ref[...], g_ref[...]).astype(BF16)

    o_ref[...] = jnp.dot(a_s[...], w_ref[...], preferred_element_type=F32).astype(o_ref.dtype)


def _norm_matmul(x, gain, w, layer, *, tm=1024, tn=1024):
    t, d = x.shape
    n = w.shape[2]
    return pl.pallas_call(
        _norm_matmul_kernel,
        grid=(t // tm, n // tn),
        in_specs=[
            pl.BlockSpec((tm, d), lambda i, j: (i, 0)),
            pl.BlockSpec((1, d), lambda i, j: (0, 0)),
            pl.BlockSpec((None, d, tn), lambda i, j: (layer, 0, j)),
        ],
        out_specs=pl.BlockSpec((tm, tn), lambda i, j: (i, j)),
        out_shape=jax.ShapeDtypeStruct((t, n), BF16),
        scratch_shapes=[pltpu.VMEM((tm, d), BF16)],
        compiler_params=_params("parallel", "arbitrary"),
        name="norm_matmul",
    )(x, gain.reshape(1, d), w)


def _matmul_resid_kernel(a_ref, w_ref, r_ref, o_ref):
    o_ref[...] = r_ref[...] + jnp.dot(a_ref[...], w_ref[...], preferred_element_type=F32)


def _matmul_resid(a, w, layer, resid, *, tm=512, tn=2048):
    t, k = a.shape
    n = w.shape[2]
    return pl.pallas_call(
        _matmul_resid_kernel,
        grid=(t // tm, n // tn),
        in_specs=[
            pl.BlockSpec((tm, k), lambda i, j: (i, 0)),
            pl.BlockSpec((None, k, tn), lambda i, j: (layer, 0, j)),
            pl.BlockSpec((tm, tn), lambda i, j: (i, j)),
        ],
        out_specs=pl.BlockSpec((tm, tn), lambda i, j: (i, j)),
        out_shape=jax.ShapeDtypeStruct((t, n), F32),
        compiler_params=_params("parallel", "arbitrary"),
        name="matmul_resid",
    )(a, w, resid)


def _ffn_kernel(h_ref, hp_ref, hn_ref, g_ref, wg_ref, wu_ref, cwg_ref, cwu_ref,
                cbg_ref, cbu_ref, wd_ref, fg_ref, o_ref, a_s, *, tm, seq_tiles, final_norm):
    i = pl.program_id(0)
    j = pl.program_id(1)
    rows = tm + HALO

    @pl.when(j == 0)
    def _():
        gain = g_ref[...]
        for c in range(0, tm, NORM_CHUNK):
            x = h_ref[c:c + NORM_CHUNK, :]
            o_ref[c:c + NORM_CHUNK, :] = x
            a_s[c:c + NORM_CHUNK, :] = _rms(x, gain).astype(BF16)
        first = (i % seq_tiles) == 0
        last = (i % seq_tiles) == seq_tiles - 1
        r = lax.broadcasted_iota(jnp.int32, hp_ref.shape, 0)
        halo = jnp.where((r == 0) & ~last, _rms(hn_ref[...], gain),
                         jnp.where((r == HALO - 1) & ~first, _rms(hp_ref[...], gain), 0.0))
        a_s[tm:rows, :] = halo.astype(BF16)

    a = a_s[...]

    def conv_branch(w_ref, cw_ref, cb_ref):
        up = jnp.dot(a, w_ref[...], preferred_element_type=F32)
        cw = cw_ref[...]
        c = (pltpu.roll(up, 1, 0) * cw[0:1] + up * cw[1:2]
             + pltpu.roll(up, rows - 1, 0) * cw[2:3])
        return c[0:tm] + cb_ref[...]

    g = conv_branch(wg_ref, cwg_ref, cbg_ref)
    u = conv_branch(wu_ref, cwu_ref, cbu_ref)
    act = (g * jax.nn.sigmoid(g) * u).astype(BF16)
    o_ref[...] += jnp.dot(act, wd_ref[...], preferred_element_type=F32)

    if final_norm:
        @pl.when(j == pl.num_programs(1) - 1)
        def _():
            for c in range(0, tm, NORM_CHUNK):
                o_ref[c:c + NORM_CHUNK, :] = _rms(o_ref[c:c + NORM_CHUNK, :], fg_ref[...])


def _ffn(h, gain, w_up, conv_w, conv_b, w_down, layer, final_gain, *, final_norm, tm=1024, tn=512):
    t, d = h.shape
    d_ff = w_down.shape[1]
    nj = d_ff // tn
    seq_tiles = SEQ // tm
    hb = tm // HALO
    n_hblocks = t // HALO
    return pl.pallas_call(
        partial(_ffn_kernel, tm=tm, seq_tiles=seq_tiles, final_norm=final_norm),
        grid=(t // tm, nj),
        in_specs=[
            pl.BlockSpec((tm, d), lambda i, j: (i, 0)),
            pl.BlockSpec((HALO, d), lambda i, j: (jnp.maximum(i * hb - 1, 0), 0)),
            pl.BlockSpec((HALO, d), lambda i, j: (jnp.minimum((i + 1) * hb, n_hblocks - 1), 0)),
            pl.BlockSpec((1, d), lambda i, j: (0, 0)),
            pl.BlockSpec((None, d, tn), lambda i, j: (layer, 0, j)),
            pl.BlockSpec((None, d, tn), lambda i, j: (layer, 0, j + nj)),
            pl.BlockSpec((None, 3, tn), lambda i, j: (layer, 0, j)),
            pl.BlockSpec((None, 3, tn), lambda i, j: (layer, 0, j + nj)),
            pl.BlockSpec((None, 1, tn), lambda i, j: (layer, 0, j)),
            pl.BlockSpec((None, 1, tn), lambda i, j: (layer, 0, j + nj)),
            pl.BlockSpec((None, tn, d), lambda i, j: (layer, j, 0)),
            pl.BlockSpec((1, d), lambda i, j: (0, 0)),
        ],
        out_specs=pl.BlockSpec((tm, d), lambda i, j: (i, 0)),
        out_shape=jax.ShapeDtypeStruct((t, d), F32),
        scratch_shapes=[pltpu.VMEM((tm + HALO, d), BF16)],
        compiler_params=_params("parallel", "arbitrary", vmem=60 * MIB),
        name="conv_glu_ffn",
    )(h, h, h, gain.reshape(1, d), w_up, w_up, conv_w, conv_w, conv_b, conv_b, w_down,
      final_gain.reshape(1, d))


def _na_bias_table(rpb):
    heads, n_dr, n_dc = rpb.shape
    period = 2 * GRID_W
    cols = jnp.arange(GRID_W)
    c_start = jnp.clip(cols - NA_WIN_C // 2, 0, GRID_W - NA_WIN_C)
    col_valid = (cols[None, :] >= c_start[:, None]) & (cols[None, :] < c_start[:, None] + NA_WIN_C)
    lo = GRID_W - NA_WIN_C
    row = jnp.pad(rpb * LOG2E, ((0, 0), (0, 0), (lo, period - lo - n_dc)))
    flat = jnp.tile(row, (1, 1, GRID_W))[:, :, GRID_W - 1:GRID_W - 1 + GRID_W * (period - 1)]
    tbl = flat.reshape(heads, n_dr, GRID_W, period - 1)[..., :GRID_W]
    tbl = jnp.where(col_valid[None, None], tbl, NEG_INF)
    tbl = jnp.transpose(tbl, (0, 2, 1, 3)).reshape(heads, GRID_W, -1)
    width = 2 * NA_WIN_R * GRID_W
    even = jnp.pad(tbl, ((0, 0), (0, 0), (0, width - tbl.shape[-1])))
    odd = jnp.pad(tbl[:, :, GRID_W:], ((0, 0), (0, 0), (0, width - tbl.shape[-1] + GRID_W)))
    return jnp.stack([even, odd], axis=1).astype(F32)


def _with_ones_column(v):
    return jnp.concatenate([v, jnp.ones_like(v)], axis=1)


def _na_kernel(q_ref, k_ref, v_ref, b_ref, o_ref, s_s, p_s, v_s, *, group):
    win = NA_WIN_R * GRID_W
    n_groups = GRID_ROWS // group
    grows = group * GRID_W
    v_s[...] = _with_ones_column(v_ref[...])

    def r_start(r):
        return min(max(r - NA_WIN_R // 2, 0), GRID_ROWS - NA_WIN_R)

    def bias(r):
        first_offset = NA_WIN_R - 1 - (r - r_start(r))
        lane0 = (first_offset // 2) * 2 * GRID_W
        return b_ref[0, first_offset % 2, :, lane0:lane0 + win]

    def scores(g):
        for r in range(g * group, (g + 1) * group):
            q0, k0 = r * GRID_W, r_start(r) * GRID_W
            q = (q_ref[q0:q0 + GRID_W, :].astype(F32) * (HEAD_DIM ** -0.5 * LOG2E)).astype(BF16)
            s = lax.dot_general(q, k_ref[k0:k0 + win, :], (((1,), (1,)), ((), ())),
                                preferred_element_type=F32)
            s_s[q0:q0 + GRID_W, :] = s + bias(r)

    def softmax(g):
        rs = slice(g * grows, (g + 1) * grows)
        s = s_s[rs, :]
        p_s[rs, :] = jnp.exp2(s - jnp.max(s, axis=-1, keepdims=True)).astype(BF16)

    def weighted_values(g):
        for r in range(g * group, (g + 1) * group):
            q0, k0 = r * GRID_W, r_start(r) * GRID_W
            o = jnp.dot(p_s[q0:q0 + GRID_W, :], v_s[k0:k0 + win, :], preferred_element_type=F32)
            o_ref[q0:q0 + GRID_W, :] = (o[:, :HEAD_DIM] / o[:, HEAD_DIM:HEAD_DIM + 1]).astype(o_ref.dtype)

    scores(0)
    for g in range(n_groups):
        if g + 1 < n_groups:
            scores(g + 1)
        softmax(g)
        weighted_values(g)


def _na_attention(qkv, bias_tbl, batch, *, group=4):
    t = qkv.shape[0]
    win = NA_WIN_R * GRID_W
    blk = lambda off: pl.BlockSpec((SEQ, HEAD_DIM), lambda h, b: (b, h + off))
    return pl.pallas_call(
        partial(_na_kernel, group=group),
        grid=(NA_HEADS, batch),
        in_specs=[
            blk(0), blk(NA_HEADS), blk(2 * NA_HEADS),
            pl.BlockSpec((1, 2, GRID_W, 2 * win), lambda h, b: (h, 0, 0, 0)),
        ],
        out_specs=pl.BlockSpec((SEQ, HEAD_DIM), lambda h, b: (b, h)),
        out_shape=jax.ShapeDtypeStruct((t, NA_HEADS * HEAD_DIM), BF16),
        scratch_shapes=[pltpu.VMEM((SEQ, win), F32), pltpu.VMEM((SEQ, win), BF16),
                        pltpu.VMEM((SEQ, 2 * HEAD_DIM), BF16)],
        compiler_params=_params("parallel", "arbitrary"),
        name="neighborhood_attention",
    )(qkv, qkv, qkv, bias_tbl)


def _sc_out_kernel(gb_ref, gc_ref, hh_ref, gcp_ref, hhp_ref, gcn_ref, hhn_ref, cw_ref,
                   w_ref, r_ref, o_ref, *, tm, seq_tiles, chunk):
    i = pl.program_id(0)
    first = (i % seq_tiles) == 0
    last = (i % seq_tiles) == seq_tiles - 1
    rows = lax.broadcasted_iota(jnp.int32, (tm, chunk), 0)
    acc = r_ref[...]
    for c in range(0, w_ref.shape[0], chunk):
        cs = slice(c, c + chunk)
        p = gc_ref[:, cs].astype(F32) * hh_ref[:, cs].astype(F32)
        prev = gcp_ref[HALO - 1:HALO, cs].astype(F32) * hhp_ref[HALO - 1:HALO, cs].astype(F32)
        nxt = gcn_ref[0:1, cs].astype(F32) * hhn_ref[0:1, cs].astype(F32)
        prev = jnp.where(first, 0.0, prev)
        nxt = jnp.where(last, 0.0, nxt)
        pm1 = jnp.where(rows == 0, prev, pltpu.roll(p, 1, 0))
        pp1 = jnp.where(rows == tm - 1, nxt, pltpu.roll(p, tm - 1, 0))
        cw = cw_ref[:, cs]
        conv = pm1 * cw[0:1] + p * cw[1:2] + pp1 * cw[2:3]
        y = (gb_ref[:, cs].astype(F32) * conv).astype(BF16)
        acc = acc + jnp.dot(y, w_ref[cs, :], preferred_element_type=F32)
    o_ref[...] = acc


def _sc_out(z, conv_w, w_out, layer, resid, *, tm=512, chunk=512):
    t = z.shape[0]
    _, d, n = w_out.shape
    seq_tiles = SEQ // tm
    hb = tm // HALO
    n_hblocks = t // HALO
    main = lambda c: pl.BlockSpec((tm, d), lambda i: (i, c))
    prev = lambda c: pl.BlockSpec((HALO, d), lambda i: (jnp.maximum(i * hb - 1, 0), c))
    nxt = lambda c: pl.BlockSpec((HALO, d), lambda i: (jnp.minimum((i + 1) * hb, n_hblocks - 1), c))
    return pl.pallas_call(
        partial(_sc_out_kernel, tm=tm, seq_tiles=seq_tiles, chunk=chunk),
        grid=(t // tm,),
        in_specs=[
            main(0), main(1), main(2), prev(1), prev(2), nxt(1), nxt(2),
            pl.BlockSpec((None, 3, d), lambda i: (layer, 0, 0)),
            pl.BlockSpec((None, d, n), lambda i: (layer, 0, 0)),
            pl.BlockSpec((tm, n), lambda i: (i, 0)),
        ],
        out_specs=pl.BlockSpec((tm, n), lambda i: (i, 0)),
        out_shape=jax.ShapeDtypeStruct((t, n), F32),
        compiler_params=_params("parallel"),
        name="short_conv_out",
    )(z, z, z, z, z, z, z, conv_w, w_out, resid)


def _rope_tables():
    t = jnp.arange(SEQ)
    row = (t // GRID_W).astype(F32)[:, None]
    col = (t % GRID_W).astype(F32)[:, None]
    half = HEAD_DIM // 2
    inv = ROPE_THETA ** (-jnp.arange(0, half, 2, dtype=F32) / half)
    ang = jnp.concatenate([row * inv, row * inv, col * inv, col * inv], axis=-1)
    cos, sin = jnp.cos(ang), jnp.sin(ang)
    lane = jnp.arange(HEAD_DIM)[None, :]
    lower = (lane % half) < (half // 2)
    sin_lo = jnp.where(lower, -sin, 0.0)
    sin_hi = jnp.where(lower, 0.0, sin)
    return cos, sin_lo, sin_hi


def _norm_rope(x, gain, cos, sin_lo, sin_hi):
    y = _rms(x, gain)
    quarter = HEAD_DIM // 4
    return (y * cos + pltpu.roll(y, HEAD_DIM - quarter, 1) * sin_lo
            + pltpu.roll(y, quarter, 1) * sin_hi)


def _gqa_kernel(q_ref, k_ref, v_ref, qn_ref, kn_ref, cq_ref, slq_ref, shq_ref,
                ck_ref, slk_ref, shk_ref, o_ref, kt_s, v_s):
    @pl.when(pl.program_id(2) == 0)
    def _():
        k = _norm_rope(k_ref[...].astype(F32), kn_ref[...], ck_ref[...], slk_ref[...], shk_ref[...])
        kt_s[...] = k.T.astype(BF16)
        v_s[...] = _with_ones_column(v_ref[...])

    cos, sin_lo, sin_hi = cq_ref[...], slq_ref[...], shq_ref[...]
    heads = [slice(g * HEAD_DIM, (g + 1) * HEAD_DIM) for g in range(GQA_GROUP)]

    def scores(hs):
        q = _norm_rope(q_ref[:, hs].astype(F32), qn_ref[...], cos, sin_lo, sin_hi)
        q = (q * (HEAD_DIM ** -0.5 * LOG2E)).astype(BF16)
        return jnp.dot(q, kt_s[...], preferred_element_type=F32)

    def weighted_values(hs, s):
        p = jnp.exp2(s - jnp.max(s, axis=-1, keepdims=True)).astype(BF16)
        o = jnp.dot(p, v_s[...], preferred_element_type=F32)
        o_ref[:, hs] = (o[:, :HEAD_DIM] / o[:, HEAD_DIM:HEAD_DIM + 1]).astype(o_ref.dtype)

    s_next = scores(heads[0])
    for g, hs in enumerate(heads):
        s = s_next
        if g + 1 < len(heads):
            s_next = scores(heads[g + 1])
        weighted_values(hs, s)


def _gqa_attention(qkv, q_norm, k_norm, batch, *, tq=256):
    t = qkv.shape[0]
    nq = SEQ // tq
    gw = GQA_GROUP * HEAD_DIM
    cos, sin_lo, sin_hi = _rope_tables()
    qtab = pl.BlockSpec((tq, HEAD_DIM), lambda b, h, i: (i, 0))
    ktab = pl.BlockSpec((SEQ, HEAD_DIM), lambda b, h, i: (0, 0))
    vec = pl.BlockSpec((1, HEAD_DIM), lambda b, h, i: (0, 0))
    return pl.pallas_call(
        _gqa_kernel,
        grid=(batch, GQA_KV_HEADS, nq),
        in_specs=[
            pl.BlockSpec((tq, gw), lambda b, h, i: (b * nq + i, h)),
            pl.BlockSpec((SEQ, HEAD_DIM), lambda b, h, i: (b, GQA_Q_HEADS + h)),
            pl.BlockSpec((SEQ, HEAD_DIM), lambda b, h, i: (b, GQA_Q_HEADS + GQA_KV_HEADS + h)),
            vec, vec, qtab, qtab, qtab, ktab, ktab, ktab,
        ],
        out_specs=pl.BlockSpec((tq, gw), lambda b, h, i: (b * nq + i, h)),
        out_shape=jax.ShapeDtypeStruct((t, GQA_Q_HEADS * HEAD_DIM), BF16),
        scratch_shapes=[pltpu.VMEM((HEAD_DIM, SEQ), BF16), pltpu.VMEM((SEQ, 2 * HEAD_DIM), BF16)],
        compiler_params=_params("parallel", "parallel", "arbitrary"),
        name="gqa_attention",
    )(qkv, qkv, qkv, q_norm.reshape(1, -1), k_norm.reshape(1, -1),
      cos, sin_lo, sin_hi, cos, sin_lo, sin_hi)


def kernel(x, mix_norm, ffn_norm, final_norm, na_w_qkv, na_rpb, na_w_o, sc_w_in, sc_conv_w,
           sc_w_out, gqa_w_qkv, gqa_q_norm, gqa_k_norm, gqa_w_o, ffn_w_up, ffn_conv_w,
           ffn_conv_b, ffn_w_down):
    batch, seq, d = x.shape
    assert (seq, d) == (SEQ, D_MODEL)
    depth = mix_norm.shape[0]
    h = x.reshape(batch * seq, d)
    bf = lambda w: w.astype(BF16)
    na_w_qkv, na_w_o, sc_w_in, sc_w_out = bf(na_w_qkv), bf(na_w_o), bf(sc_w_in), bf(sc_w_out)
    gqa_w_qkv, gqa_w_o, ffn_w_up, ffn_w_down = bf(gqa_w_qkv), bf(gqa_w_o), bf(ffn_w_up), bf(ffn_w_down)
    ffn_conv_b = ffn_conv_b.reshape(depth, 1, -1)
    for i in range(depth):
        m, j = i % N_MIXERS, i // N_MIXERS
        if m == 0:
            qkv = _norm_matmul(h, mix_norm[i], na_w_qkv, j)
            att = _na_attention(qkv, _na_bias_table(na_rpb[j]), batch)
            h = _matmul_resid(att, na_w_o, j, h)
        elif m == 1:
            z = _norm_matmul(h, mix_norm[i], sc_w_in, j)
            h = _sc_out(z, sc_conv_w, sc_w_out, j, h)
        else:
            qkv = _norm_matmul(h, mix_norm[i], gqa_w_qkv, j)
            att = _gqa_attention(qkv, gqa_q_norm[j], gqa_k_norm[j], batch)
            h = _matmul_resid(att, gqa_w_o, j, h)
        h = _ffn(h, ffn_norm[i], ffn_w_up, ffn_conv_w, ffn_conv_b, ffn_w_down, i, final_norm,
                 final_norm=(i == depth - 1))
    return h.reshape(batch, seq, d)
```

```python
from functools import partial

import jax
import jax.numpy as jnp
from jax import lax
from jax.experimental import pallas as pl
from jax.experimental.pallas import tpu as pltpu

D_MODEL = 2048
SEQ = 2048
GRID_W = 64
GRID_ROWS = SEQ // GRID_W
HEAD_DIM = 128
EPS = 1e-6
NEG_INF = -1e30
NA_HEADS = D_MODEL // HEAD_DIM
NA_WIN_R = 8
NA_WIN_C = 16
NA_BLOCK_ROWS = 2
NA_PAD_ROWS = NA_BLOCK_ROWS - 1
GQA_Q_HEADS = D_MODEL // HEAD_DIM
GQA_KV_HEADS = GQA_Q_HEADS // 4
GQA_GROUP = GQA_Q_HEADS // GQA_KV_HEADS
ROPE_THETA = 10000.0
N_MIXERS = 3

HALO = 16
LANES = 128
NA_TABLE_LANES = -(-(2 * NA_WIN_R - 1 + 2 * NA_PAD_ROWS) * GRID_W // LANES) * LANES
NORM_CHUNK = 64
LOG2E = 1.4426950408889634
MIB = 1024 * 1024
VMEM_LIMIT = 48 * MIB

BF16 = jnp.bfloat16
F32 = jnp.float32


def _params(*sem, vmem=VMEM_LIMIT):
    return pltpu.CompilerParams(dimension_semantics=sem, vmem_limit_bytes=vmem)


def _rms(x, gain):
    return x * lax.rsqrt(jnp.mean(x * x, axis=-1, keepdims=True) + EPS) * gain


def _norm_matmul_kernel(x_ref, g_ref, w_ref, o_ref, a_s):
    @pl.when(pl.program_id(1) == 0)
    def _():
        for c in range(0, a_s.shape[0], NORM_CHUNK):
            a_s[c:c + NORM_CHUNK, :] = _rms(x_ref[c:c + NORM_CHUNK, :], g_ref[...]).astype(BF16)

    w = w_ref[...].astype(BF16)
    o_ref[...] = jnp.dot(a_s[...], w, preferred_element_type=F32).astype(o_ref.dtype)


def _norm_matmul(x, gain, w, layer, *, tm=1024, tn=1024):
    t, d = x.shape
    n = w.shape[2]
    return pl.pallas_call(
        _norm_matmul_kernel,
        grid=(t // tm, n // tn),
        in_specs=[
            pl.BlockSpec((tm, d), lambda i, j: (i, 0)),
            pl.BlockSpec((1, d), lambda i, j: (0, 0)),
            pl.BlockSpec((None, d, tn), lambda i, j: (layer, 0, j)),
        ],
        out_specs=pl.BlockSpec((tm, tn), lambda i, j: (i, j)),
        out_shape=jax.ShapeDtypeStruct((t, n), BF16),
        scratch_shapes=[pltpu.VMEM((tm, d), BF16)],
        compiler_params=_params("parallel", "arbitrary", vmem=56 * MIB),
        name="norm_matmul",
    )(x, gain.reshape(1, d), w)


def _matmul_resid_kernel(a_ref, w_ref, r_ref, o_ref):
    o_ref[...] = r_ref[...] + jnp.dot(a_ref[...], w_ref[...], preferred_element_type=F32)


def _matmul_resid(a, w, layer, resid, *, tm=512, tn=2048):
    t, k = a.shape
    n = w.shape[2]
    return pl.pallas_call(
        _matmul_resid_kernel,
        grid=(t // tm, n // tn),
        in_specs=[
            pl.BlockSpec((tm, k), lambda i, j: (i, 0)),
            pl.BlockSpec((None, k, tn), lambda i, j: (layer, 0, j)),
            pl.BlockSpec((tm, tn), lambda i, j: (i, j)),
        ],
        out_specs=pl.BlockSpec((tm, tn), lambda i, j: (i, j)),
        out_shape=jax.ShapeDtypeStruct((t, n), F32),
        compiler_params=_params("parallel", "arbitrary"),
        name="matmul_resid",
    )(a, w, resid)


def _ffn_kernel(h_ref, hp_ref, hn_ref, g_ref, wg_ref, wu_ref, cwg_ref, cwu_ref,
                cbg_ref, cbu_ref, wd_ref, fg_ref, o_ref, a_s, *, tm, seq_tiles, final_norm):
    i = pl.program_id(0)
    j = pl.program_id(1)
    rows = tm + HALO

    @pl.when(j == 0)
    def _():
        gain = g_ref[...]
        for c in range(0, tm, NORM_CHUNK):
            x = h_ref[c:c + NORM_CHUNK, :]
            o_ref[c:c + NORM_CHUNK, :] = x
            a_s[c:c + NORM_CHUNK, :] = _rms(x, gain).astype(BF16)
        first = (i % seq_tiles) == 0
        last = (i % seq_tiles) == seq_tiles - 1
        r = lax.broadcasted_iota(jnp.int32, hp_ref.shape, 0)
        halo = jnp.where((r == 0) & ~last, _rms(hn_ref[...], gain),
                         jnp.where((r == HALO - 1) & ~first, _rms(hp_ref[...], gain), 0.0))
        a_s[tm:rows, :] = halo.astype(BF16)

    a = a_s[...]

    def conv_branch(w_ref, cw_ref, cb_ref):
        up = jnp.dot(a, w_ref[...], preferred_element_type=F32)
        cw = cw_ref[...]
        c = (pltpu.roll(up, 1, 0) * cw[0:1] + up * cw[1:2]
             + pltpu.roll(up, rows - 1, 0) * cw[2:3])
        return c[0:tm] + cb_ref[...]

    g = conv_branch(wg_ref, cwg_ref, cbg_ref)
    u = conv_branch(wu_ref, cwu_ref, cbu_ref)
    act = (g * jax.nn.sigmoid(g) * u).astype(BF16)
    o_ref[...] += jnp.dot(act, wd_ref[...], preferred_element_type=F32)

    if final_norm:
        @pl.when(j == pl.num_programs(1) - 1)
        def _():
            for c in range(0, tm, NORM_CHUNK):
                o_ref[c:c + NORM_CHUNK, :] = _rms(o_ref[c:c + NORM_CHUNK, :], fg_ref[...])


def _ffn(h, gain, w_up, conv_w, conv_b, w_down, layer, final_gain, *, final_norm, tm=1024, tn=512):
    t, d = h.shape
    d_ff = w_down.shape[1]
    nj = d_ff // tn
    seq_tiles = SEQ // tm
    hb = tm // HALO
    n_hblocks = t // HALO
    return pl.pallas_call(
        partial(_ffn_kernel, tm=tm, seq_tiles=seq_tiles, final_norm=final_norm),
        grid=(t // tm, nj),
        in_specs=[
            pl.BlockSpec((tm, d), lambda i, j: (i, 0)),
            pl.BlockSpec((HALO, d), lambda i, j: (jnp.maximum(i * hb - 1, 0), 0)),
            pl.BlockSpec((HALO, d), lambda i, j: (jnp.minimum((i + 1) * hb, n_hblocks - 1), 0)),
            pl.BlockSpec((1, d), lambda i, j: (0, 0)),
            pl.BlockSpec((None, d, tn), lambda i, j: (layer, 0, j)),
            pl.BlockSpec((None, d, tn), lambda i, j: (layer, 0, j + nj)),
            pl.BlockSpec((None, 3, tn), lambda i, j: (layer, 0, j)),
            pl.BlockSpec((None, 3, tn), lambda i, j: (layer, 0, j + nj)),
            pl.BlockSpec((None, 1, tn), lambda i, j: (layer, 0, j)),
            pl.BlockSpec((None, 1, tn), lambda i, j: (layer, 0, j + nj)),
            pl.BlockSpec((None, tn, d), lambda i, j: (layer, j, 0)),
            pl.BlockSpec((1, d), lambda i, j: (0, 0)),
        ],
        out_specs=pl.BlockSpec((tm, d), lambda i, j: (i, 0)),
        out_shape=jax.ShapeDtypeStruct((t, d), F32),
        scratch_shapes=[pltpu.VMEM((tm + HALO, d), BF16)],
        compiler_params=_params("parallel", "arbitrary", vmem=60 * MIB),
        name="conv_glu_ffn",
    )(h, h, h, gain.reshape(1, d), w_up, w_up, conv_w, conv_w, conv_b, conv_b, w_down,
      final_gain.reshape(1, d))


def _na_bias_table(rpb):
    heads, n_dr, n_dc = rpb.shape
    period = 2 * GRID_W
    cols = jnp.arange(GRID_W)
    c_start = jnp.clip(cols - NA_WIN_C // 2, 0, GRID_W - NA_WIN_C)
    col_valid = (cols[None, :] >= c_start[:, None]) & (cols[None, :] < c_start[:, None] + NA_WIN_C)
    lo = GRID_W - NA_WIN_C
    row = jnp.pad(rpb * LOG2E, ((0, 0), (0, 0), (lo, period - lo - n_dc)))
    flat = jnp.tile(row, (1, 1, GRID_W))[:, :, GRID_W - 1:GRID_W - 1 + GRID_W * (period - 1)]
    tbl = flat.reshape(heads, n_dr, GRID_W, period - 1)[..., :GRID_W]
    tbl = jnp.where(col_valid[None, None], tbl, NEG_INF)
    tbl = jnp.transpose(tbl, (0, 2, 1, 3)).reshape(heads, GRID_W, -1)
    left = NA_PAD_ROWS * GRID_W
    pad = lambda x, l: jnp.pad(x, ((0, 0), (0, 0), (l, NA_TABLE_LANES - l - x.shape[-1])),
                               constant_values=NEG_INF)
    return jnp.stack([pad(tbl, left), pad(tbl, left - GRID_W)], axis=1).astype(F32)


def _with_ones_column(v):
    return jnp.concatenate([v, jnp.ones_like(v)], axis=1)


def _na_kernel(q_ref, k_ref, v_ref, b_ref, o_ref, s_s, p_s, v_s, *, group):
    union_rows = NA_WIN_R + NA_BLOCK_ROWS - 1
    union = union_rows * GRID_W
    bq = NA_BLOCK_ROWS * GRID_W
    n_groups = GRID_ROWS // (NA_BLOCK_ROWS * group)
    v_s[...] = _with_ones_column(v_ref[...])

    def r_start(r):
        return min(max(r - NA_WIN_R // 2, 0), GRID_ROWS - NA_WIN_R)

    def k_start(blk):
        return min(r_start(blk * NA_BLOCK_ROWS), GRID_ROWS - union_rows)

    def row_bias(r, start):
        first = NA_WIN_R - 1 + NA_PAD_ROWS - (r - start)
        lane0 = (first // 2) * 2 * GRID_W
        b = b_ref[0, first % 2, :, lane0:lane0 + union]
        lo, hi = (r_start(r) - start) * GRID_W, (r_start(r) - start + NA_WIN_R) * GRID_W
        tiles = []
        for t0 in range(0, union, LANES):
            t1 = min(t0 + LANES, union)
            tile = b[:, t0:t1]
            if t1 <= lo or t0 >= hi:
                tile = jnp.full_like(tile, NEG_INF)
            elif t0 < lo or t1 > hi:
                lane = t0 + lax.broadcasted_iota(jnp.int32, tile.shape, 1)
                tile = jnp.where((lane >= lo) & (lane < hi), tile, NEG_INF)
            tiles.append(tile)
        return jnp.concatenate(tiles, axis=1)

    def blocks(g):
        return range(g * group, (g + 1) * group)

    def scores(g):
        for blk in blocks(g):
            q0, start = blk * bq, k_start(blk)
            q = (q_ref[q0:q0 + bq, :].astype(F32) * (HEAD_DIM ** -0.5 * LOG2E)).astype(BF16)
            s = lax.dot_general(q, k_ref[start * GRID_W:start * GRID_W + union, :],
                                (((1,), (1,)), ((), ())), preferred_element_type=F32)
            bias = jnp.concatenate([row_bias(blk * NA_BLOCK_ROWS + i, start)
                                    for i in range(NA_BLOCK_ROWS)], axis=0)
            s_s[q0:q0 + bq, :] = s + bias

    def softmax(g):
        rs = slice(g * group * bq, (g + 1) * group * bq)
        s = s_s[rs, :]
        p_s[rs, :] = jnp.exp2(s - jnp.max(s, axis=-1, keepdims=True)).astype(BF16)

    def weighted_values(g):
        for blk in blocks(g):
            q0, k0 = blk * bq, k_start(blk) * GRID_W
            o = jnp.dot(p_s[q0:q0 + bq, :], v_s[k0:k0 + union, :], preferred_element_type=F32)
            o_ref[q0:q0 + bq, :] = (o[:, :HEAD_DIM] / o[:, HEAD_DIM:HEAD_DIM + 1]).astype(o_ref.dtype)

    scores(0)
    for g in range(n_groups):
        if g + 1 < n_groups:
            scores(g + 1)
        softmax(g)
        weighted_values(g)


def _na_attention(qkv, bias_tbl, batch, *, group=4):
    t = qkv.shape[0]
    union = (NA_WIN_R + NA_BLOCK_ROWS - 1) * GRID_W
    blk = lambda off: pl.BlockSpec((SEQ, HEAD_DIM), lambda h, b: (b, h + off))
    return pl.pallas_call(
        partial(_na_kernel, group=group),
        grid=(NA_HEADS, batch),
        in_specs=[
            blk(0), blk(NA_HEADS), blk(2 * NA_HEADS),
            pl.BlockSpec((1, 2, GRID_W, NA_TABLE_LANES), lambda h, b: (h, 0, 0, 0)),
        ],
        out_specs=pl.BlockSpec((SEQ, HEAD_DIM), lambda h, b: (b, h)),
        out_shape=jax.ShapeDtypeStruct((t, NA_HEADS * HEAD_DIM), BF16),
        scratch_shapes=[pltpu.VMEM((SEQ, union), F32), pltpu.VMEM((SEQ, union), BF16),
                        pltpu.VMEM((SEQ, 2 * HEAD_DIM), BF16)],
        compiler_params=_params("parallel", "arbitrary"),
        name="neighborhood_attention",
    )(qkv, qkv, qkv, bias_tbl)


def _sc_out_kernel(gb_ref, gc_ref, hh_ref, gcp_ref, hhp_ref, gcn_ref, hhn_ref, cw_ref,
                   w_ref, r_ref, o_ref, *, tm, seq_tiles, chunk):
    i = pl.program_id(0)
    first = (i % seq_tiles) == 0
    last = (i % seq_tiles) == seq_tiles - 1
    rows = lax.broadcasted_iota(jnp.int32, (tm, chunk), 0)
    acc = r_ref[...]
    for c in range(0, w_ref.shape[0], chunk):
        cs = slice(c, c + chunk)
        p = gc_ref[:, cs].astype(F32) * hh_ref[:, cs].astype(F32)
        prev = gcp_ref[HALO - 1:HALO, cs].astype(F32) * hhp_ref[HALO - 1:HALO, cs].astype(F32)
        nxt = gcn_ref[0:1, cs].astype(F32) * hhn_ref[0:1, cs].astype(F32)
        prev = jnp.where(first, 0.0, prev)
        nxt = jnp.where(last, 0.0, nxt)
        pm1 = jnp.where(rows == 0, prev, pltpu.roll(p, 1, 0))
        pp1 = jnp.where(rows == tm - 1, nxt, pltpu.roll(p, tm - 1, 0))
        cw = cw_ref[:, cs]
        conv = pm1 * cw[0:1] + p * cw[1:2] + pp1 * cw[2:3]
        y = (gb_ref[:, cs].astype(F32) * conv).astype(BF16)
        acc = acc + jnp.dot(y, w_ref[cs, :], preferred_element_type=F32)
    o_ref[...] = acc


def _sc_out(z, conv_w, w_out, layer, resid, *, tm=512, chunk=512):
    t = z.shape[0]
    _, d, n = w_out.shape
    seq_tiles = SEQ // tm
    hb = tm // HALO
    n_hblocks = t // HALO
    main = lambda c: pl.BlockSpec((tm, d), lambda i: (i, c))
    prev = lambda c: pl.BlockSpec((HALO, d), lambda i: (jnp.maximum(i * hb - 1, 0), c))
    nxt = lambda c: pl.BlockSpec((HALO, d), lambda i: (jnp.minimum((i + 1) * hb, n_hblocks - 1), c))
    return pl.pallas_call(
        partial(_sc_out_kernel, tm=tm, seq_tiles=seq_tiles, chunk=chunk),
        grid=(t // tm,),
        in_specs=[
            main(0), main(1), main(2), prev(1), prev(2), nxt(1), nxt(2),
            pl.BlockSpec((None, 3, d), lambda i: (layer, 0, 0)),
            pl.BlockSpec((None, d, n), lambda i: (layer, 0, 0)),
            pl.BlockSpec((tm, n), lambda i: (i, 0)),
        ],
        out_specs=pl.BlockSpec((tm, n), lambda i: (i, 0)),
        out_shape=jax.ShapeDtypeStruct((t, n), F32),
        compiler_params=_params("parallel"),
        name="short_conv_out",
    )(z, z, z, z, z, z, z, conv_w, w_out, resid)


def _rope_tables():
    t = jnp.arange(SEQ)
    row = (t // GRID_W).astype(F32)[:, None]
    col = (t % GRID_W).astype(F32)[:, None]
    half = HEAD_DIM // 2
    inv = ROPE_THETA ** (-jnp.arange(0, half, 2, dtype=F32) / half)
    ang = jnp.concatenate([row * inv, row * inv, col * inv, col * inv], axis=-1)
    cos, sin = jnp.cos(ang), jnp.sin(ang)
    lane = jnp.arange(HEAD_DIM)[None, :]
    lower = (lane % half) < (half // 2)
    sin_lo = jnp.where(lower, -sin, 0.0)
    sin_hi = jnp.where(lower, 0.0, sin)
    return cos, sin_lo, sin_hi


def _norm_rope(x, gain, cos, sin_lo, sin_hi):
    y = _rms(x, gain)
    quarter = HEAD_DIM // 4
    return (y * cos + pltpu.roll(y, HEAD_DIM - quarter, 1) * sin_lo
            + pltpu.roll(y, quarter, 1) * sin_hi)


def _gqa_kernel(q_ref, k_ref, v_ref, qn_ref, kn_ref, cq_ref, slq_ref, shq_ref,
                ck_ref, slk_ref, shk_ref, o_ref, kt_s, v_s, *, row_parts):
    @pl.when(pl.program_id(2) == 0)
    def _():
        k = _norm_rope(k_ref[...].astype(F32), kn_ref[...], ck_ref[...], slk_ref[...], shk_ref[...])
        kt_s[...] = k.T.astype(BF16)
        v_s[...] = _with_ones_column(v_ref[...])

    cos, sin_lo, sin_hi = cq_ref[...], slq_ref[...], shq_ref[...]
    tq = q_ref.shape[0]
    units = [(slice(r, r + tq // row_parts), slice(g * HEAD_DIM, (g + 1) * HEAD_DIM))
             for g in range(GQA_GROUP) for r in range(0, tq, tq // row_parts)]

    def scores(rs, hs):
        q = _norm_rope(q_ref[rs, hs].astype(F32), qn_ref[...], cos[rs], sin_lo[rs], sin_hi[rs])
        q = (q * (HEAD_DIM ** -0.5 * LOG2E)).astype(BF16)
        return jnp.dot(q, kt_s[...], preferred_element_type=F32)

    def weighted_values(rs, hs, s):
        p = jnp.exp2(s - jnp.max(s, axis=-1, keepdims=True)).astype(BF16)
        o = jnp.dot(p, v_s[...], preferred_element_type=F32)
        o_ref[rs, hs] = (o[:, :HEAD_DIM] / o[:, HEAD_DIM:HEAD_DIM + 1]).astype(o_ref.dtype)

    s_next = scores(*units[0])
    for n, unit in enumerate(units):
        s = s_next
        if n + 1 < len(units):
            s_next = scores(*units[n + 1])
        weighted_values(*unit, s)


def _gqa_attention(qkv, q_norm, k_norm, batch, *, tq=1024, row_parts=4):
    t = qkv.shape[0]
    nq = SEQ // tq
    gw = GQA_GROUP * HEAD_DIM
    cos, sin_lo, sin_hi = _rope_tables()
    qtab = pl.BlockSpec((tq, HEAD_DIM), lambda b, h, i: (i, 0))
    ktab = pl.BlockSpec((SEQ, HEAD_DIM), lambda b, h, i: (0, 0))
    vec = pl.BlockSpec((1, HEAD_DIM), lambda b, h, i: (0, 0))
    return pl.pallas_call(
        partial(_gqa_kernel, row_parts=row_parts),
        grid=(batch, GQA_KV_HEADS, nq),
        in_specs=[
            pl.BlockSpec((tq, gw), lambda b, h, i: (b * nq + i, h)),
            pl.BlockSpec((SEQ, HEAD_DIM), lambda b, h, i: (b, GQA_Q_HEADS + h)),
            pl.BlockSpec((SEQ, HEAD_DIM), lambda b, h, i: (b, GQA_Q_HEADS + GQA_KV_HEADS + h)),
            vec, vec, qtab, qtab, qtab, ktab, ktab, ktab,
        ],
        out_specs=pl.BlockSpec((tq, gw), lambda b, h, i: (b * nq + i, h)),
        out_shape=jax.ShapeDtypeStruct((t, GQA_Q_HEADS * HEAD_DIM), BF16),
        scratch_shapes=[pltpu.VMEM((HEAD_DIM, SEQ), BF16), pltpu.VMEM((SEQ, 2 * HEAD_DIM), BF16)],
        compiler_params=_params("parallel", "parallel", "arbitrary"),
        name="gqa_attention",
    )(qkv, qkv, qkv, q_norm.reshape(1, -1), k_norm.reshape(1, -1),
      cos, sin_lo, sin_hi, cos, sin_lo, sin_hi)


def kernel(x, mix_norm, ffn_norm, final_norm, na_w_qkv, na_rpb, na_w_o, sc_w_in, sc_conv_w,
           sc_w_out, gqa_w_qkv, gqa_q_norm, gqa_k_norm, gqa_w_o, ffn_w_up, ffn_conv_w,
           ffn_conv_b, ffn_w_down):
    batch, seq, d = x.shape
    assert (seq, d) == (SEQ, D_MODEL)
    depth = mix_norm.shape[0]
    h = x.reshape(batch * seq, d)
    bf = lambda w: w.astype(BF16)
    na_w_o, sc_w_out, gqa_w_o = bf(na_w_o), bf(sc_w_out), bf(gqa_w_o)
    ffn_w_up, ffn_w_down = bf(ffn_w_up), bf(ffn_w_down)
    ffn_conv_b = ffn_conv_b.reshape(depth, 1, -1)
    for i in range(depth):
        m, j = i % N_MIXERS, i // N_MIXERS
        if m == 0:
            qkv = _norm_matmul(h, mix_norm[i], na_w_qkv, j)
            att = _na_attention(qkv, _na_bias_table(na_rpb[j]), batch)
            h = _matmul_resid(att, na_w_o, j, h)
        elif m == 1:
            z = _norm_matmul(h, mix_norm[i], sc_w_in, j)
            h = _sc_out(z, sc_conv_w, sc_w_out, j, h)
        else:
            qkv = _norm_matmul(h, mix_norm[i], gqa_w_qkv, j)
            att = _gqa_attention(qkv, gqa_q_norm[j], gqa_k_norm[j], batch)
            h = _matmul_resid(att, gqa_w_o, j, h)
        h = _ffn(h, ffn_norm[i], ffn_w_up, ffn_conv_w, ffn_conv_b, ffn_w_down, i, final_norm,
                 final_norm=(i == depth - 1))
    return h.reshape(batch, seq, d)
```

```python
from functools import partial

import jax
import jax.numpy as jnp
from jax import lax
from jax.experimental import pallas as pl
from jax.experimental.pallas import tpu as pltpu

D_MODEL = 2048
SEQ = 2048
GRID_W = 64
GRID_ROWS = SEQ // GRID_W
HEAD_DIM = 128
EPS = 1e-6
NEG_INF = -1e30
NA_HEADS = D_MODEL // HEAD_DIM
NA_WIN_R = 8
NA_WIN_C = 16
NA_BLOCK_ROWS = 2
NA_PAD_ROWS = NA_BLOCK_ROWS - 1
GQA_Q_HEADS = D_MODEL // HEAD_DIM
GQA_KV_HEADS = GQA_Q_HEADS // 4
GQA_GROUP = GQA_Q_HEADS // GQA_KV_HEADS
ROPE_THETA = 10000.0
N_MIXERS = 3

HALO = 16
LANES = 128
NA_TABLE_LANES = -(-(2 * NA_WIN_R - 1 + 2 * NA_PAD_ROWS) * GRID_W // LANES) * LANES
NORM_CHUNK = 64
LOG2E = 1.4426950408889634
MIB = 1024 * 1024
VMEM_LIMIT = 48 * MIB

BF16 = jnp.bfloat16
F32 = jnp.float32


def _params(*sem, vmem=VMEM_LIMIT):
    return pltpu.CompilerParams(dimension_semantics=sem, vmem_limit_bytes=vmem)


def _rms(x, gain):
    return x * lax.rsqrt(jnp.mean(x * x, axis=-1, keepdims=True) + EPS) * gain


def _norm_matmul_kernel(x_ref, g_ref, w_ref, o_ref, a_s):
    @pl.when(pl.program_id(1) == 0)
    def _():
        for c in range(0, a_s.shape[0], NORM_CHUNK):
            a_s[c:c + NORM_CHUNK, :] = _rms(x_ref[c:c + NORM_CHUNK, :], g_ref[...]).astype(BF16)

    o_ref[...] = jnp.dot(a_s[...], w_ref[...], preferred_element_type=F32).astype(o_ref.dtype)


def _norm_matmul(x, gain, w, layer, *, tm=1024, tn=1024):
    t, d = x.shape
    n = w.shape[2]
    return pl.pallas_call(
        _norm_matmul_kernel,
        grid=(t // tm, n // tn),
        in_specs=[
            pl.BlockSpec((tm, d), lambda i, j: (i, 0)),
            pl.BlockSpec((1, d), lambda i, j: (0, 0)),
            pl.BlockSpec((None, d, tn), lambda i, j: (layer, 0, j)),
        ],
        out_specs=pl.BlockSpec((tm, tn), lambda i, j: (i, j)),
        out_shape=jax.ShapeDtypeStruct((t, n), BF16),
        scratch_shapes=[pltpu.VMEM((tm, d), BF16)],
        compiler_params=_params("parallel", "arbitrary"),
        name="norm_matmul",
    )(x, gain.reshape(1, d), w)


def _matmul_resid_kernel(a_ref, w_ref, r_ref, o_ref):
    o_ref[...] = r_ref[...] + jnp.dot(a_ref[...], w_ref[...], preferred_element_type=F32)


def _matmul_resid(a, w, layer, resid, *, tm=512, tn=2048):
    t, k = a.shape
    n = w.shape[2]
    return pl.pallas_call(
        _matmul_resid_kernel,
        grid=(t // tm, n // tn),
        in_specs=[
            pl.BlockSpec((tm, k), lambda i, j: (i, 0)),
            pl.BlockSpec((None, k, tn), lambda i, j: (layer, 0, j)),
            pl.BlockSpec((tm, tn), lambda i, j: (i, j)),
        ],
        out_specs=pl.BlockSpec((tm, tn), lambda i, j: (i, j)),
        out_shape=jax.ShapeDtypeStruct((t, n), F32),
        compiler_params=_params("parallel", "arbitrary"),
        name="matmul_resid",
    )(a, w, resid)


def _ffn_kernel(h_ref, hp_ref, hn_ref, g_ref, wg_ref, wu_ref, cwg_ref, cwu_ref,
                cbg_ref, cbu_ref, wd_ref, fg_ref, o_ref, a_s, *, tm, seq_tiles, final_norm):
    i = pl.program_id(0)
    j = pl.program_id(1)
    rows = tm + HALO

    @pl.when(j == 0)
    def _():
        gain = g_ref[...]
        for c in range(0, tm, NORM_CHUNK):
            x = h_ref[c:c + NORM_CHUNK, :]
            o_ref[c:c + NORM_CHUNK, :] = x
            a_s[c:c + NORM_CHUNK, :] = _rms(x, gain).astype(BF16)
        first = (i % seq_tiles) == 0
        last = (i % seq_tiles) == seq_tiles - 1
        r = lax.broadcasted_iota(jnp.int32, hp_ref.shape, 0)
        halo = jnp.where((r == 0) & ~last, _rms(hn_ref[...], gain),
                         jnp.where((r == HALO - 1) & ~first, _rms(hp_ref[...], gain), 0.0))
        a_s[tm:rows, :] = halo.astype(BF16)

    a = a_s[...]

    def conv_branch(w_ref, cw_ref, cb_ref):
        up = jnp.dot(a, w_ref[...], preferred_element_type=F32)
        cw = cw_ref[...]
        c = (pltpu.roll(up, 1, 0) * cw[0:1] + up * cw[1:2]
             + pltpu.roll(up, rows - 1, 0) * cw[2:3])
        return c[0:tm] + cb_ref[...]

    g = conv_branch(wg_ref, cwg_ref, cbg_ref)
    u = conv_branch(wu_ref, cwu_ref, cbu_ref)
    half_g = 0.5 * g
    act = ((half_g + half_g * jnp.tanh(half_g)) * u).astype(BF16)
    o_ref[...] += jnp.dot(act, wd_ref[...], preferred_element_type=F32)

    if final_norm:
        @pl.when(j == pl.num_programs(1) - 1)
        def _():
            for c in range(0, tm, NORM_CHUNK):
                o_ref[c:c + NORM_CHUNK, :] = _rms(o_ref[c:c + NORM_CHUNK, :], fg_ref[...])


def _ffn(h, gain, w_up, conv_w, conv_b, w_down, layer, final_gain, *, final_norm, tm=1024, tn=512):
    t, d = h.shape
    d_ff = w_down.shape[1]
    nj = d_ff // tn
    seq_tiles = SEQ // tm
    hb = tm // HALO
    n_hblocks = t // HALO
    return pl.pallas_call(
        partial(_ffn_kernel, tm=tm, seq_tiles=seq_tiles, final_norm=final_norm),
        grid=(t // tm, nj),
        in_specs=[
            pl.BlockSpec((tm, d), lambda i, j: (i, 0)),
            pl.BlockSpec((HALO, d), lambda i, j: (jnp.maximum(i * hb - 1, 0), 0)),
            pl.BlockSpec((HALO, d), lambda i, j: (jnp.minimum((i + 1) * hb, n_hblocks - 1), 0)),
            pl.BlockSpec((1, d), lambda i, j: (0, 0)),
            pl.BlockSpec((None, d, tn), lambda i, j: (layer, 0, j)),
            pl.BlockSpec((None, d, tn), lambda i, j: (layer, 0, j + nj)),
            pl.BlockSpec((None, 3, tn), lambda i, j: (layer, 0, j)),
            pl.BlockSpec((None, 3, tn), lambda i, j: (layer, 0, j + nj)),
            pl.BlockSpec((None, 1, tn), lambda i, j: (layer, 0, j)),
            pl.BlockSpec((None, 1, tn), lambda i, j: (layer, 0, j + nj)),
            pl.BlockSpec((None, tn, d), lambda i, j: (layer, j, 0)),
            pl.BlockSpec((1, d), lambda i, j: (0, 0)),
        ],
        out_specs=pl.BlockSpec((tm, d), lambda i, j: (i, 0)),
        out_shape=jax.ShapeDtypeStruct((t, d), F32),
        scratch_shapes=[pltpu.VMEM((tm + HALO, d), BF16)],
        compiler_params=_params("parallel", "arbitrary", vmem=60 * MIB),
        name="conv_glu_ffn",
    )(h, h, h, gain.reshape(1, d), w_up, w_up, conv_w, conv_w, conv_b, conv_b, w_down,
      final_gain.reshape(1, d))


def _na_bias_table(rpb):
    heads, n_dr, n_dc = rpb.shape
    period = 2 * GRID_W
    cols = jnp.arange(GRID_W)
    c_start = jnp.clip(cols - NA_WIN_C // 2, 0, GRID_W - NA_WIN_C)
    col_valid = (cols[None, :] >= c_start[:, None]) & (cols[None, :] < c_start[:, None] + NA_WIN_C)
    lo = GRID_W - NA_WIN_C
    row = jnp.pad(rpb * LOG2E, ((0, 0), (0, 0), (lo, period - lo - n_dc)))
    flat = jnp.tile(row, (1, 1, GRID_W))[:, :, GRID_W - 1:GRID_W - 1 + GRID_W * (period - 1)]
    tbl = flat.reshape(heads, n_dr, GRID_W, period - 1)[..., :GRID_W]
    tbl = jnp.where(col_valid[None, None], tbl, NEG_INF)
    tbl = jnp.transpose(tbl, (0, 2, 1, 3)).reshape(heads, GRID_W, -1)
    left = NA_PAD_ROWS * GRID_W
    pad = lambda x, l: jnp.pad(x, ((0, 0), (0, 0), (l, NA_TABLE_LANES - l - x.shape[-1])),
                               constant_values=NEG_INF)
    return jnp.stack([pad(tbl, left), pad(tbl, left - GRID_W)], axis=1).astype(F32)


def _with_ones_column(v):
    return jnp.concatenate([v, jnp.ones_like(v)], axis=1)


def _na_kernel(q_ref, k_ref, v_ref, b_ref, o_ref, s_s, p_s, v_s, *, group):
    union_rows = NA_WIN_R + NA_BLOCK_ROWS - 1
    union = union_rows * GRID_W
    bq = NA_BLOCK_ROWS * GRID_W
    n_groups = GRID_ROWS // (NA_BLOCK_ROWS * group)
    v_s[...] = _with_ones_column(v_ref[...])

    def r_start(r):
        return min(max(r - NA_WIN_R // 2, 0), GRID_ROWS - NA_WIN_R)

    def k_start(blk):
        return min(r_start(blk * NA_BLOCK_ROWS), GRID_ROWS - union_rows)

    def row_bias(r, start):
        first = NA_WIN_R - 1 + NA_PAD_ROWS - (r - start)
        lane0 = (first // 2) * 2 * GRID_W
        b = b_ref[0, first % 2, :, lane0:lane0 + union]
        lo, hi = (r_start(r) - start) * GRID_W, (r_start(r) - start + NA_WIN_R) * GRID_W
        tiles = []
        for t0 in range(0, union, LANES):
            t1 = min(t0 + LANES, union)
            tile = b[:, t0:t1]
            if t1 <= lo or t0 >= hi:
                tile = jnp.full_like(tile, NEG_INF)
            elif t0 < lo or t1 > hi:
                lane = t0 + lax.broadcasted_iota(jnp.int32, tile.shape, 1)
                tile = jnp.where((lane >= lo) & (lane < hi), tile, NEG_INF)
            tiles.append(tile)
        return jnp.concatenate(tiles, axis=1)

    def blocks(g):
        return range(g * group, (g + 1) * group)

    def scores(g):
        for blk in blocks(g):
            q0, start = blk * bq, k_start(blk)
            q = (q_ref[q0:q0 + bq, :].astype(F32) * (HEAD_DIM ** -0.5 * LOG2E)).astype(BF16)
            s = lax.dot_general(q, k_ref[start * GRID_W:start * GRID_W + union, :],
                                (((1,), (1,)), ((), ())), preferred_element_type=F32)
            bias = jnp.concatenate([row_bias(blk * NA_BLOCK_ROWS + i, start)
                                    for i in range(NA_BLOCK_ROWS)], axis=0)
            s_s[q0:q0 + bq, :] = s + bias

    def softmax(g):
        rs = slice(g * group * bq, (g + 1) * group * bq)
        s = s_s[rs, :]
        p_s[rs, :] = jnp.exp2(s - jnp.max(s, axis=-1, keepdims=True)).astype(BF16)

    def weighted_values(g):
        for blk in blocks(g):
            q0, k0 = blk * bq, k_start(blk) * GRID_W
            o = jnp.dot(p_s[q0:q0 + bq, :], v_s[k0:k0 + union, :], preferred_element_type=F32)
            o_ref[q0:q0 + bq, :] = (o[:, :HEAD_DIM] / o[:, HEAD_DIM:HEAD_DIM + 1]).astype(o_ref.dtype)

    scores(0)
    for g in range(n_groups):
        if g + 1 < n_groups:
            scores(g + 1)
        softmax(g)
        weighted_values(g)


def _na_attention(qkv, bias_tbl, batch, *, group=4):
    t = qkv.shape[0]
    union = (NA_WIN_R + NA_BLOCK_ROWS - 1) * GRID_W
    blk = lambda off: pl.BlockSpec((SEQ, HEAD_DIM), lambda h, b: (b, h + off))
    return pl.pallas_call(
        partial(_na_kernel, group=group),
        grid=(NA_HEADS, batch),
        in_specs=[
            blk(0), blk(NA_HEADS), blk(2 * NA_HEADS),
            pl.BlockSpec((1, 2, GRID_W, NA_TABLE_LANES), lambda h, b: (h, 0, 0, 0)),
        ],
        out_specs=pl.BlockSpec((SEQ, HEAD_DIM), lambda h, b: (b, h)),
        out_shape=jax.ShapeDtypeStruct((t, NA_HEADS * HEAD_DIM), BF16),
        scratch_shapes=[pltpu.VMEM((SEQ, union), F32), pltpu.VMEM((SEQ, union), BF16),
                        pltpu.VMEM((SEQ, 2 * HEAD_DIM), BF16)],
        compiler_params=_params("parallel", "arbitrary"),
        name="neighborhood_attention",
    )(qkv, qkv, qkv, bias_tbl)


def _sc_out_kernel(gb_ref, gc_ref, hh_ref, gcp_ref, hhp_ref, gcn_ref, hhn_ref, cw_ref,
                   w_ref, r_ref, o_ref, *, tm, seq_tiles, chunk):
    i = pl.program_id(0)
    first = (i % seq_tiles) == 0
    last = (i % seq_tiles) == seq_tiles - 1
    rows = lax.broadcasted_iota(jnp.int32, (tm, chunk), 0)
    acc = r_ref[...]
    for c in range(0, w_ref.shape[0], chunk):
        cs = slice(c, c + chunk)
        p = gc_ref[:, cs].astype(F32) * hh_ref[:, cs].astype(F32)
        prev = gcp_ref[HALO - 1:HALO, cs].astype(F32) * hhp_ref[HALO - 1:HALO, cs].astype(F32)
        nxt = gcn_ref[0:1, cs].astype(F32) * hhn_ref[0:1, cs].astype(F32)
        prev = jnp.where(first, 0.0, prev)
        nxt = jnp.where(last, 0.0, nxt)
        pm1 = jnp.where(rows == 0, prev, pltpu.roll(p, 1, 0))
        pp1 = jnp.where(rows == tm - 1, nxt, pltpu.roll(p, tm - 1, 0))
        cw = cw_ref[:, cs]
        conv = pm1 * cw[0:1] + p * cw[1:2] + pp1 * cw[2:3]
        y = (gb_ref[:, cs].astype(F32) * conv).astype(BF16)
        acc = acc + jnp.dot(y, w_ref[cs, :], preferred_element_type=F32)
    o_ref[...] = acc


def _sc_out(z, conv_w, w_out, layer, resid, *, tm=512, chunk=512):
    t = z.shape[0]
    _, d, n = w_out.shape
    seq_tiles = SEQ // tm
    hb = tm // HALO
    n_hblocks = t // HALO
    main = lambda c: pl.BlockSpec((tm, d), lambda i: (i, c))
    prev = lambda c: pl.BlockSpec((HALO, d), lambda i: (jnp.maximum(i * hb - 1, 0), c))
    nxt = lambda c: pl.BlockSpec((HALO, d), lambda i: (jnp.minimum((i + 1) * hb, n_hblocks - 1), c))
    return pl.pallas_call(
        partial(_sc_out_kernel, tm=tm, seq_tiles=seq_tiles, chunk=chunk),
        grid=(t // tm,),
        in_specs=[
            main(0), main(1), main(2), prev(1), prev(2), nxt(1), nxt(2),
            pl.BlockSpec((None, 3, d), lambda i: (layer, 0, 0)),
            pl.BlockSpec((None, d, n), lambda i: (layer, 0, 0)),
            pl.BlockSpec((tm, n), lambda i: (i, 0)),
        ],
        out_specs=pl.BlockSpec((tm, n), lambda i: (i, 0)),
        out_shape=jax.ShapeDtypeStruct((t, n), F32),
        compiler_params=_params("parallel"),
        name="short_conv_out",
    )(z, z, z, z, z, z, z, conv_w, w_out, resid)


def _rope_tables():
    t = jnp.arange(SEQ)
    row = (t // GRID_W).astype(F32)[:, None]
    col = (t % GRID_W).astype(F32)[:, None]
    half = HEAD_DIM // 2
    inv = ROPE_THETA ** (-jnp.arange(0, half, 2, dtype=F32) / half)
    ang = jnp.concatenate([row * inv, row * inv, col * inv, col * inv], axis=-1)
    cos, sin = jnp.cos(ang), jnp.sin(ang)
    lane = jnp.arange(HEAD_DIM)[None, :]
    lower = (lane % half) < (half // 2)
    sin_lo = jnp.where(lower, -sin, 0.0)
    sin_hi = jnp.where(lower, 0.0, sin)
    return cos, sin_lo, sin_hi


def _norm_rope(x, gain, cos, sin_lo, sin_hi):
    y = _rms(x, gain)
    quarter = HEAD_DIM // 4
    return (y * cos + pltpu.roll(y, HEAD_DIM - quarter, 1) * sin_lo
            + pltpu.roll(y, quarter, 1) * sin_hi)


def _gqa_kernel(q_ref, k_ref, v_ref, qn_ref, kn_ref, cq_ref, slq_ref, shq_ref,
                ck_ref, slk_ref, shk_ref, o_ref, kt_s, v_s, *, row_parts):
    @pl.when(pl.program_id(2) == 0)
    def _():
        k = _norm_rope(k_ref[...].astype(F32), kn_ref[...], ck_ref[...], slk_ref[...], shk_ref[...])
        kt_s[...] = k.T.astype(BF16)
        v_s[...] = _with_ones_column(v_ref[...])

    cos, sin_lo, sin_hi = cq_ref[...], slq_ref[...], shq_ref[...]
    tq = q_ref.shape[0]
    units = [(slice(r, r + tq // row_parts), slice(g * HEAD_DIM, (g + 1) * HEAD_DIM))
             for g in range(GQA_GROUP) for r in range(0, tq, tq // row_parts)]

    def scores(rs, hs):
        q = _norm_rope(q_ref[rs, hs].astype(F32), qn_ref[...], cos[rs], sin_lo[rs], sin_hi[rs])
        q = (q * (HEAD_DIM ** -0.5 * LOG2E)).astype(BF16)
        return jnp.dot(q, kt_s[...], preferred_element_type=F32)

    def weighted_values(rs, hs, s):
        p = jnp.exp2(s - jnp.max(s, axis=-1, keepdims=True)).astype(BF16)
        o = jnp.dot(p, v_s[...], preferred_element_type=F32)
        o_ref[rs, hs] = (o[:, :HEAD_DIM] / o[:, HEAD_DIM:HEAD_DIM + 1]).astype(o_ref.dtype)

    s_next = scores(*units[0])
    for n, unit in enumerate(units):
        s = s_next
        if n + 1 < len(units):
            s_next = scores(*units[n + 1])
        weighted_values(*unit, s)


def _gqa_attention(qkv, q_norm, k_norm, batch, *, tq=1024, row_parts=4):
    t = qkv.shape[0]
    nq = SEQ // tq
    gw = GQA_GROUP * HEAD_DIM
    cos, sin_lo, sin_hi = _rope_tables()
    qtab = pl.BlockSpec((tq, HEAD_DIM), lambda b, h, i: (i, 0))
    ktab = pl.BlockSpec((SEQ, HEAD_DIM), lambda b, h, i: (0, 0))
    vec = pl.BlockSpec((1, HEAD_DIM), lambda b, h, i: (0, 0))
    return pl.pallas_call(
        partial(_gqa_kernel, row_parts=row_parts),
        grid=(batch, GQA_KV_HEADS, nq),
        in_specs=[
            pl.BlockSpec((tq, gw), lambda b, h, i: (b * nq + i, h)),
            pl.BlockSpec((SEQ, HEAD_DIM), lambda b, h, i: (b, GQA_Q_HEADS + h)),
            pl.BlockSpec((SEQ, HEAD_DIM), lambda b, h, i: (b, GQA_Q_HEADS + GQA_KV_HEADS + h)),
            vec, vec, qtab, qtab, qtab, ktab, ktab, ktab,
        ],
        out_specs=pl.BlockSpec((tq, gw), lambda b, h, i: (b * nq + i, h)),
        out_shape=jax.ShapeDtypeStruct((t, GQA_Q_HEADS * HEAD_DIM), BF16),
        scratch_shapes=[pltpu.VMEM((HEAD_DIM, SEQ), BF16), pltpu.VMEM((SEQ, 2 * HEAD_DIM), BF16)],
        compiler_params=_params("parallel", "parallel", "arbitrary"),
        name="gqa_attention",
    )(qkv, qkv, qkv, q_norm.reshape(1, -1), k_norm.reshape(1, -1),
      cos, sin_lo, sin_hi, cos, sin_lo, sin_hi)


def kernel(x, mix_norm, ffn_norm, final_norm, na_w_qkv, na_rpb, na_w_o, sc_w_in, sc_conv_w,
           sc_w_out, gqa_w_qkv, gqa_q_norm, gqa_k_norm, gqa_w_o, ffn_w_up, ffn_conv_w,
           ffn_conv_b, ffn_w_down):
    batch, seq, d = x.shape
    assert (seq, d) == (SEQ, D_MODEL)
    depth = mix_norm.shape[0]
    h = x.reshape(batch * seq, d)
    bf = lambda w: w.astype(BF16)
    na_w_qkv, na_w_o, sc_w_in, sc_w_out = bf(na_w_qkv), bf(na_w_o), bf(sc_w_in), bf(sc_w_out)
    gqa_w_qkv, gqa_w_o, ffn_w_up, ffn_w_down = bf(gqa_w_qkv), bf(gqa_w_o), bf(ffn_w_up), bf(ffn_w_down)
    ffn_conv_b = ffn_conv_b.reshape(depth, 1, -1)
    for i in range(depth):
        m, j = i % N_MIXERS, i // N_MIXERS
        if m == 0:
            qkv = _norm_matmul(h, mix_norm[i], na_w_qkv, j)
            att = _na_attention(qkv, _na_bias_table(na_rpb[j]), batch)
            h = _matmul_resid(att, na_w_o, j, h)
        elif m == 1:
            z = _norm_matmul(h, mix_norm[i], sc_w_in, j)
            h = _sc_out(z, sc_conv_w, sc_w_out, j, h)
        else:
            qkv = _norm_matmul(h, mix_norm[i], gqa_w_qkv, j)
            att = _gqa_attention(qkv, gqa_q_norm[j], gqa_k_norm[j], batch)
            h = _matmul_resid(att, gqa_w_o, j, h)
        h = _ffn(h, ffn_norm[i], ffn_w_up, ffn_conv_w, ffn_conv_b, ffn_w_down, i, final_norm,
                 final_norm=(i == depth - 1))
    return h.reshape(batch, seq, d)
```

```python
from functools import partial

import jax
import jax.numpy as jnp
from jax import lax
from jax.experimental import pallas as pl
from jax.experimental.pallas import tpu as pltpu

D_MODEL = 2048
SEQ = 2048
GRID_W = 64
GRID_ROWS = SEQ // GRID_W
HEAD_DIM = 128
EPS = 1e-6
NEG_INF = -1e30
NA_HEADS = D_MODEL // HEAD_DIM
NA_WIN_R = 8
NA_WIN_C = 16
NA_BLOCK_ROWS = 2
NA_PAD_ROWS = NA_BLOCK_ROWS - 1
NA_HEADS_PER_STEP = 2
GQA_Q_HEADS = D_MODEL // HEAD_DIM
GQA_KV_HEADS = GQA_Q_HEADS // 4
GQA_GROUP = GQA_Q_HEADS // GQA_KV_HEADS
ROPE_THETA = 10000.0
N_MIXERS = 3

HALO = 16
LANES = 128
NA_TABLE_LANES = -(-(2 * NA_WIN_R - 1 + 2 * NA_PAD_ROWS) * GRID_W // LANES) * LANES
NORM_CHUNK = 64
LOG2E = 1.4426950408889634
MIB = 1024 * 1024
VMEM_LIMIT = 48 * MIB

BF16 = jnp.bfloat16
F32 = jnp.float32


def _params(*sem, vmem=VMEM_LIMIT):
    return pltpu.CompilerParams(dimension_semantics=sem, vmem_limit_bytes=vmem)


def _rms(x, gain):
    return x * lax.rsqrt(jnp.mean(x * x, axis=-1, keepdims=True) + EPS) * gain


def _norm_matmul_kernel(x_ref, g_ref, w_ref, o_ref, a_s):
    @pl.when(pl.program_id(1) == 0)
    def _():
        for c in range(0, a_s.shape[0], NORM_CHUNK):
            a_s[c:c + NORM_CHUNK, :] = _rms(x_ref[c:c + NORM_CHUNK, :], g_ref[...]).astype(BF16)

    o_ref[...] = jnp.dot(a_s[...], w_ref[...], preferred_element_type=F32).astype(o_ref.dtype)


def _norm_matmul(x, gain, w, layer, *, tm=1024, tn=1024):
    t, d = x.shape
    n = w.shape[2]
    return pl.pallas_call(
        _norm_matmul_kernel,
        grid=(t // tm, n // tn),
        in_specs=[
            pl.BlockSpec((tm, d), lambda i, j: (i, 0)),
            pl.BlockSpec((1, d), lambda i, j: (0, 0)),
            pl.BlockSpec((None, d, tn), lambda i, j: (layer, 0, j)),
        ],
        out_specs=pl.BlockSpec((tm, tn), lambda i, j: (i, j)),
        out_shape=jax.ShapeDtypeStruct((t, n), BF16),
        scratch_shapes=[pltpu.VMEM((tm, d), BF16)],
        compiler_params=_params("parallel", "arbitrary"),
        name="norm_matmul",
    )(x, gain.reshape(1, d), w)


def _matmul_resid_kernel(a_ref, w_ref, r_ref, o_ref):
    o_ref[...] = r_ref[...] + jnp.dot(a_ref[...], w_ref[...], preferred_element_type=F32)


def _matmul_resid(a, w, layer, resid, *, tm=512, tn=2048):
    t, k = a.shape
    n = w.shape[2]
    return pl.pallas_call(
        _matmul_resid_kernel,
        grid=(t // tm, n // tn),
        in_specs=[
            pl.BlockSpec((tm, k), lambda i, j: (i, 0)),
            pl.BlockSpec((None, k, tn), lambda i, j: (layer, 0, j)),
            pl.BlockSpec((tm, tn), lambda i, j: (i, j)),
        ],
        out_specs=pl.BlockSpec((tm, tn), lambda i, j: (i, j)),
        out_shape=jax.ShapeDtypeStruct((t, n), F32),
        compiler_params=_params("parallel", "arbitrary"),
        name="matmul_resid",
    )(a, w, resid)


def _ffn_kernel(h_ref, hp_ref, hn_ref, g_ref, wg_ref, wu_ref, cwg_ref, cwu_ref,
                cbg_ref, cbu_ref, wd_ref, fg_ref, o_ref, a_s, *, tm, seq_tiles, final_norm):
    i = pl.program_id(0)
    j = pl.program_id(1)
    rows = tm + HALO

    @pl.when(j == 0)
    def _():
        gain = g_ref[...]
        for c in range(0, tm, NORM_CHUNK):
            x = h_ref[c:c + NORM_CHUNK, :]
            o_ref[c:c + NORM_CHUNK, :] = x
            a_s[c:c + NORM_CHUNK, :] = _rms(x, gain).astype(BF16)
        first = (i % seq_tiles) == 0
        last = (i % seq_tiles) == seq_tiles - 1
        r = lax.broadcasted_iota(jnp.int32, hp_ref.shape, 0)
        halo = jnp.where((r == 0) & ~last, _rms(hn_ref[...], gain),
                         jnp.where((r == HALO - 1) & ~first, _rms(hp_ref[...], gain), 0.0))
        a_s[tm:rows, :] = halo.astype(BF16)

    a = a_s[...]

    def conv_branch(w_ref, cw_ref, cb_ref):
        up = jnp.dot(a, w_ref[...], preferred_element_type=F32)
        cw = cw_ref[...]
        c = (pltpu.roll(up, 1, 0) * cw[0:1] + up * cw[1:2]
             + pltpu.roll(up, rows - 1, 0) * cw[2:3])
        return c[0:tm] + cb_ref[...]

    g = conv_branch(wg_ref, cwg_ref, cbg_ref)
    u = conv_branch(wu_ref, cwu_ref, cbu_ref)
    half_g = 0.5 * g
    act = ((half_g + half_g * jnp.tanh(half_g)) * u).astype(BF16)
    o_ref[...] += jnp.dot(act, wd_ref[...], preferred_element_type=F32)

    if final_norm:
        @pl.when(j == pl.num_programs(1) - 1)
        def _():
            for c in range(0, tm, NORM_CHUNK):
                o_ref[c:c + NORM_CHUNK, :] = _rms(o_ref[c:c + NORM_CHUNK, :], fg_ref[...])


def _ffn(h, gain, w_up, conv_w, conv_b, w_down, layer, final_gain, *, final_norm, tm=1024, tn=512):
    t, d = h.shape
    d_ff = w_down.shape[1]
    nj = d_ff // tn
    seq_tiles = SEQ // tm
    hb = tm // HALO
    n_hblocks = t // HALO
    return pl.pallas_call(
        partial(_ffn_kernel, tm=tm, seq_tiles=seq_tiles, final_norm=final_norm),
        grid=(t // tm, nj),
        in_specs=[
            pl.BlockSpec((tm, d), lambda i, j: (i, 0)),
            pl.BlockSpec((HALO, d), lambda i, j: (jnp.maximum(i * hb - 1, 0), 0)),
            pl.BlockSpec((HALO, d), lambda i, j: (jnp.minimum((i + 1) * hb, n_hblocks - 1), 0)),
            pl.BlockSpec((1, d), lambda i, j: (0, 0)),
            pl.BlockSpec((None, d, tn), lambda i, j: (layer, 0, j)),
            pl.BlockSpec((None, d, tn), lambda i, j: (layer, 0, j + nj)),
            pl.BlockSpec((None, 3, tn), lambda i, j: (layer, 0, j)),
            pl.BlockSpec((None, 3, tn), lambda i, j: (layer, 0, j + nj)),
            pl.BlockSpec((None, 1, tn), lambda i, j: (layer, 0, j)),
            pl.BlockSpec((None, 1, tn), lambda i, j: (layer, 0, j + nj)),
            pl.BlockSpec((None, tn, d), lambda i, j: (layer, j, 0)),
            pl.BlockSpec((1, d), lambda i, j: (0, 0)),
        ],
        out_specs=pl.BlockSpec((tm, d), lambda i, j: (i, 0)),
        out_shape=jax.ShapeDtypeStruct((t, d), F32),
        scratch_shapes=[pltpu.VMEM((tm + HALO, d), BF16)],
        compiler_params=_params("parallel", "arbitrary", vmem=60 * MIB),
        name="conv_glu_ffn",
    )(h, h, h, gain.reshape(1, d), w_up, w_up, conv_w, conv_w, conv_b, conv_b, w_down,
      final_gain.reshape(1, d))


def _na_bias_table(rpb):
    heads, n_dr, n_dc = rpb.shape
    period = 2 * GRID_W
    cols = jnp.arange(GRID_W)
    c_start = jnp.clip(cols - NA_WIN_C // 2, 0, GRID_W - NA_WIN_C)
    col_valid = (cols[None, :] >= c_start[:, None]) & (cols[None, :] < c_start[:, None] + NA_WIN_C)
    lo = GRID_W - NA_WIN_C
    row = jnp.pad(rpb * LOG2E, ((0, 0), (0, 0), (lo, period - lo - n_dc)))
    flat = jnp.tile(row, (1, 1, GRID_W))[:, :, GRID_W - 1:GRID_W - 1 + GRID_W * (period - 1)]
    tbl = flat.reshape(heads, n_dr, GRID_W, period - 1)[..., :GRID_W]
    tbl = jnp.where(col_valid[None, None], tbl, NEG_INF)
    tbl = jnp.transpose(tbl, (0, 2, 1, 3)).reshape(heads, GRID_W, -1)
    left = NA_PAD_ROWS * GRID_W
    pad = lambda x, l: jnp.pad(x, ((0, 0), (0, 0), (l, NA_TABLE_LANES - l - x.shape[-1])),
                               constant_values=NEG_INF)
    return jnp.stack([pad(tbl, left), pad(tbl, left - GRID_W)], axis=1).astype(F32)


def _with_ones_column(v):
    return jnp.concatenate([v, jnp.ones_like(v)], axis=1)


def _na_kernel(q_ref, k_ref, v_ref, b_ref, o_ref, s_s, p_s, v_s, *, group):
    union_rows = NA_WIN_R + NA_BLOCK_ROWS - 1
    union = union_rows * GRID_W
    bq = NA_BLOCK_ROWS * GRID_W
    n_groups = GRID_ROWS // (NA_BLOCK_ROWS * group)
    lanes = [slice(hd * HEAD_DIM, (hd + 1) * HEAD_DIM) for hd in range(NA_HEADS_PER_STEP)]
    for hd, ls in enumerate(lanes):
        v_s[hd] = _with_ones_column(v_ref[:, ls])

    def r_start(r):
        return min(max(r - NA_WIN_R // 2, 0), GRID_ROWS - NA_WIN_R)

    def k_start(blk):
        return min(r_start(blk * NA_BLOCK_ROWS), GRID_ROWS - union_rows)

    def row_bias(hd, r, start):
        first = NA_WIN_R - 1 + NA_PAD_ROWS - (r - start)
        lane0 = (first // 2) * 2 * GRID_W
        b = b_ref[hd, first % 2, :, lane0:lane0 + union]
        lo, hi = (r_start(r) - start) * GRID_W, (r_start(r) - start + NA_WIN_R) * GRID_W
        tiles = []
        for t0 in range(0, union, LANES):
            t1 = min(t0 + LANES, union)
            tile = b[:, t0:t1]
            if t1 <= lo or t0 >= hi:
                tile = jnp.full_like(tile, NEG_INF)
            elif t0 < lo or t1 > hi:
                lane = t0 + lax.broadcasted_iota(jnp.int32, tile.shape, 1)
                tile = jnp.where((lane >= lo) & (lane < hi), tile, NEG_INF)
            tiles.append(tile)
        return jnp.concatenate(tiles, axis=1)

    def blocks(g):
        return range(g * group, (g + 1) * group)

    def scores(hd, g):
        for blk in blocks(g):
            q0, start = blk * bq, k_start(blk)
            q = (q_ref[q0:q0 + bq, lanes[hd]].astype(F32) * (HEAD_DIM ** -0.5 * LOG2E)).astype(BF16)
            s = lax.dot_general(q, k_ref[start * GRID_W:start * GRID_W + union, lanes[hd]],
                                (((1,), (1,)), ((), ())), preferred_element_type=F32)
            bias = jnp.concatenate([row_bias(hd, blk * NA_BLOCK_ROWS + i, start)
                                    for i in range(NA_BLOCK_ROWS)], axis=0)
            s_s[hd, q0:q0 + bq, :] = s + bias

    def softmax(hd, g):
        rs = slice(g * group * bq, (g + 1) * group * bq)
        s = s_s[hd, rs, :]
        p_s[hd, rs, :] = jnp.exp2(s - jnp.max(s, axis=-1, keepdims=True)).astype(BF16)

    def weighted_values(hd, g):
        for blk in blocks(g):
            q0, k0 = blk * bq, k_start(blk) * GRID_W
            o = jnp.dot(p_s[hd, q0:q0 + bq, :], v_s[hd, k0:k0 + union, :],
                        preferred_element_type=F32)
            o_ref[q0:q0 + bq, lanes[hd]] = (
                o[:, :HEAD_DIM] / o[:, HEAD_DIM:HEAD_DIM + 1]).astype(o_ref.dtype)

    stages = [(hd, g) for hd in range(NA_HEADS_PER_STEP) for g in range(n_groups)]
    scores(*stages[0])
    for n, stage in enumerate(stages):
        if n + 1 < len(stages):
            scores(*stages[n + 1])
        softmax(*stage)
        weighted_values(*stage)


def _na_attention(qkv, bias_tbl, batch, *, group=4):
    t = qkv.shape[0]
    union = (NA_WIN_R + NA_BLOCK_ROWS - 1) * GRID_W
    hps = NA_HEADS_PER_STEP
    n_steps = NA_HEADS // hps
    blk = lambda off: pl.BlockSpec((SEQ, hps * HEAD_DIM), lambda h, b: (b, h + off))
    return pl.pallas_call(
        partial(_na_kernel, group=group),
        grid=(n_steps, batch),
        in_specs=[
            blk(0), blk(n_steps), blk(2 * n_steps),
            pl.BlockSpec((hps, 2, GRID_W, NA_TABLE_LANES), lambda h, b: (h, 0, 0, 0)),
        ],
        out_specs=pl.BlockSpec((SEQ, hps * HEAD_DIM), lambda h, b: (b, h)),
        out_shape=jax.ShapeDtypeStruct((t, NA_HEADS * HEAD_DIM), BF16),
        scratch_shapes=[pltpu.VMEM((hps, SEQ, union), F32), pltpu.VMEM((hps, SEQ, union), BF16),
                        pltpu.VMEM((hps, SEQ, 2 * HEAD_DIM), BF16)],
        compiler_params=_params("parallel", "arbitrary"),
        name="neighborhood_attention",
    )(qkv, qkv, qkv, bias_tbl)


def _sc_out_kernel(gb_ref, gc_ref, hh_ref, gcp_ref, hhp_ref, gcn_ref, hhn_ref, cw_ref,
                   w_ref, r_ref, o_ref, *, tm, seq_tiles, chunk):
    i = pl.program_id(0)
    first = (i % seq_tiles) == 0
    last = (i % seq_tiles) == seq_tiles - 1
    rows = lax.broadcasted_iota(jnp.int32, (tm, chunk), 0)
    acc = r_ref[...]
    for c in range(0, w_ref.shape[0], chunk):
        cs = slice(c, c + chunk)
        p = gc_ref[:, cs].astype(F32) * hh_ref[:, cs].astype(F32)
        prev = gcp_ref[HALO - 1:HALO, cs].astype(F32) * hhp_ref[HALO - 1:HALO, cs].astype(F32)
        nxt = gcn_ref[0:1, cs].astype(F32) * hhn_ref[0:1, cs].astype(F32)
        prev = jnp.where(first, 0.0, prev)
        nxt = jnp.where(last, 0.0, nxt)
        pm1 = jnp.where(rows == 0, prev, pltpu.roll(p, 1, 0))
        pp1 = jnp.where(rows == tm - 1, nxt, pltpu.roll(p, tm - 1, 0))
        cw = cw_ref[:, cs]
        conv = pm1 * cw[0:1] + p * cw[1:2] + pp1 * cw[2:3]
        y = (gb_ref[:, cs].astype(F32) * conv).astype(BF16)
        acc = acc + jnp.dot(y, w_ref[cs, :], preferred_element_type=F32)
    o_ref[...] = acc


def _sc_out(z, conv_w, w_out, layer, resid, *, tm=512, chunk=512):
    t = z.shape[0]
    _, d, n = w_out.shape
    seq_tiles = SEQ // tm
    hb = tm // HALO
    n_hblocks = t // HALO
    main = lambda c: pl.BlockSpec((tm, d), lambda i: (i, c))
    prev = lambda c: pl.BlockSpec((HALO, d), lambda i: (jnp.maximum(i * hb - 1, 0), c))
    nxt = lambda c: pl.BlockSpec((HALO, d), lambda i: (jnp.minimum((i + 1) * hb, n_hblocks - 1), c))
    return pl.pallas_call(
        partial(_sc_out_kernel, tm=tm, seq_tiles=seq_tiles, chunk=chunk),
        grid=(t // tm,),
        in_specs=[
            main(0), main(1), main(2), prev(1), prev(2), nxt(1), nxt(2),
            pl.BlockSpec((None, 3, d), lambda i: (layer, 0, 0)),
            pl.BlockSpec((None, d, n), lambda i: (layer, 0, 0)),
            pl.BlockSpec((tm, n), lambda i: (i, 0)),
        ],
        out_specs=pl.BlockSpec((tm, n), lambda i: (i, 0)),
        out_shape=jax.ShapeDtypeStruct((t, n), F32),
        compiler_params=_params("parallel"),
        name="short_conv_out",
    )(z, z, z, z, z, z, z, conv_w, w_out, resid)


def _rope_tables():
    t = jnp.arange(SEQ)
    row = (t // GRID_W).astype(F32)[:, None]
    col = (t % GRID_W).astype(F32)[:, None]
    half = HEAD_DIM // 2
    inv = ROPE_THETA ** (-jnp.arange(0, half, 2, dtype=F32) / half)
    ang = jnp.concatenate([row * inv, row * inv, col * inv, col * inv], axis=-1)
    cos, sin = jnp.cos(ang), jnp.sin(ang)
    lane = jnp.arange(HEAD_DIM)[None, :]
    lower = (lane % half) < (half // 2)
    sin_lo = jnp.where(lower, -sin, 0.0)
    sin_hi = jnp.where(lower, 0.0, sin)
    return cos, sin_lo, sin_hi


def _norm_rope(x, gain, cos, sin_lo, sin_hi):
    y = _rms(x, gain)
    quarter = HEAD_DIM // 4
    return (y * cos + pltpu.roll(y, HEAD_DIM - quarter, 1) * sin_lo
            + pltpu.roll(y, quarter, 1) * sin_hi)


def _gqa_kernel(q_ref, k_ref, v_ref, qn_ref, kn_ref, cq_ref, slq_ref, shq_ref,
                ck_ref, slk_ref, shk_ref, o_ref, kt_s, v_s, *, row_parts):
    @pl.when(pl.program_id(2) == 0)
    def _():
        k = _norm_rope(k_ref[...].astype(F32), kn_ref[...], ck_ref[...], slk_ref[...], shk_ref[...])
        kt_s[...] = k.T.astype(BF16)
        v_s[...] = _with_ones_column(v_ref[...])

    cos, sin_lo, sin_hi = cq_ref[...], slq_ref[...], shq_ref[...]
    tq = q_ref.shape[0]
    units = [(slice(r, r + tq // row_parts), slice(g * HEAD_DIM, (g + 1) * HEAD_DIM))
             for g in range(GQA_GROUP) for r in range(0, tq, tq // row_parts)]

    def scores(rs, hs):
        q = _norm_rope(q_ref[rs, hs].astype(F32), qn_ref[...], cos[rs], sin_lo[rs], sin_hi[rs])
        q = (q * (HEAD_DIM ** -0.5 * LOG2E)).astype(BF16)
        return jnp.dot(q, kt_s[...], preferred_element_type=F32)

    def weighted_values(rs, hs, s):
        p = jnp.exp2(s - jnp.max(s, axis=-1, keepdims=True)).astype(BF16)
        o = jnp.dot(p, v_s[...], preferred_element_type=F32)
        o_ref[rs, hs] = (o[:, :HEAD_DIM] / o[:, HEAD_DIM:HEAD_DIM + 1]).astype(o_ref.dtype)

    s_next = scores(*units[0])
    for n, unit in enumerate(units):
        s = s_next
        if n + 1 < len(units):
            s_next = scores(*units[n + 1])
        weighted_values(*unit, s)


def _gqa_attention(qkv, q_norm, k_norm, batch, *, tq=1024, row_parts=4):
    t = qkv.shape[0]
    nq = SEQ // tq
    gw = GQA_GROUP * HEAD_DIM
    cos, sin_lo, sin_hi = _rope_tables()
    qtab = pl.BlockSpec((tq, HEAD_DIM), lambda b, h, i: (i, 0))
    ktab = pl.BlockSpec((SEQ, HEAD_DIM), lambda b, h, i: (0, 0))
    vec = pl.BlockSpec((1, HEAD_DIM), lambda b, h, i: (0, 0))
    return pl.pallas_call(
        partial(_gqa_kernel, row_parts=row_parts),
        grid=(batch, GQA_KV_HEADS, nq),
        in_specs=[
            pl.BlockSpec((tq, gw), lambda b, h, i: (b * nq + i, h)),
            pl.BlockSpec((SEQ, HEAD_DIM), lambda b, h, i: (b, GQA_Q_HEADS + h)),
            pl.BlockSpec((SEQ, HEAD_DIM), lambda b, h, i: (b, GQA_Q_HEADS + GQA_KV_HEADS + h)),
            vec, vec, qtab, qtab, qtab, ktab, ktab, ktab,
        ],
        out_specs=pl.BlockSpec((tq, gw), lambda b, h, i: (b * nq + i, h)),
        out_shape=jax.ShapeDtypeStruct((t, GQA_Q_HEADS * HEAD_DIM), BF16),
        scratch_shapes=[pltpu.VMEM((HEAD_DIM, SEQ), BF16), pltpu.VMEM((SEQ, 2 * HEAD_DIM), BF16)],
        compiler_params=_params("parallel", "parallel", "arbitrary"),
        name="gqa_attention",
    )(qkv, qkv, qkv, q_norm.reshape(1, -1), k_norm.reshape(1, -1),
      cos, sin_lo, sin_hi, cos, sin_lo, sin_hi)


def kernel(x, mix_norm, ffn_norm, final_norm, na_w_qkv, na_rpb, na_w_o, sc_w_in, sc_conv_w,
           sc_w_out, gqa_w_qkv, gqa_q_norm, gqa_k_norm, gqa_w_o, ffn_w_up, ffn_conv_w,
           ffn_conv_b, ffn_w_down):
    batch, seq, d = x.shape
    assert (seq, d) == (SEQ, D_MODEL)
    depth = mix_norm.shape[0]
    h = x.reshape(batch * seq, d)
    bf = lambda w: w.astype(BF16)
    na_w_qkv, na_w_o, sc_w_in, sc_w_out = bf(na_w_qkv), bf(na_w_o), bf(sc_w_in), bf(sc_w_out)
    gqa_w_qkv, gqa_w_o, ffn_w_up, ffn_w_down = bf(gqa_w_qkv), bf(gqa_w_o), bf(ffn_w_up), bf(ffn_w_down)
    ffn_conv_b = ffn_conv_b.reshape(depth, 1, -1)
    for i in range(depth):
        m, j = i % N_MIXERS, i // N_MIXERS
        if m == 0:
            qkv = _norm_matmul(h, mix_norm[i], na_w_qkv, j)
            att = _na_attention(qkv, _na_bias_table(na_rpb[j]), batch)
            h = _matmul_resid(att, na_w_o, j, h)
        elif m == 1:
            z = _norm_matmul(h, mix_norm[i], sc_w_in, j)
            h = _sc_out(z, sc_conv_w, sc_w_out, j, h)
        else:
            qkv = _norm_matmul(h, mix_norm[i], gqa_w_qkv, j)
            att = _gqa_attention(qkv, gqa_q_norm[j], gqa_k_norm[j], batch)
            h = _matmul_resid(att, gqa_w_o, j, h)
        h = _ffn(h, ffn_norm[i], ffn_w_up, ffn_conv_w, ffn_conv_b, ffn_w_down, i, final_norm,
                 final_norm=(i == depth - 1))
    return h.reshape(batch, seq, d)
```

```python
from functools import partial

import jax
import jax.numpy as jnp
from jax import lax
from jax.experimental import pallas as pl
from jax.experimental.pallas import tpu as pltpu

D_MODEL = 2048
SEQ = 2048
GRID_W = 64
GRID_ROWS = SEQ // GRID_W
HEAD_DIM = 128
EPS = 1e-6
NEG_INF = -1e30
NA_HEADS = D_MODEL // HEAD_DIM
NA_WIN_R = 8
NA_WIN_C = 16
NA_BLOCK_ROWS = 2
NA_PAD_ROWS = NA_BLOCK_ROWS - 1
NA_HEADS_PER_STEP = 2
GQA_Q_HEADS = D_MODEL // HEAD_DIM
GQA_KV_HEADS = GQA_Q_HEADS // 4
GQA_GROUP = GQA_Q_HEADS // GQA_KV_HEADS
ROPE_THETA = 10000.0
N_MIXERS = 3

HALO = 16
LANES = 128
NA_TABLE_LANES = -(-(2 * NA_WIN_R - 1 + 2 * NA_PAD_ROWS) * GRID_W // LANES) * LANES
NORM_CHUNK = 64
LOG2E = 1.4426950408889634
MIB = 1024 * 1024
VMEM_LIMIT = 48 * MIB

BF16 = jnp.bfloat16
F32 = jnp.float32


def _params(*sem, vmem=VMEM_LIMIT):
    return pltpu.CompilerParams(dimension_semantics=sem, vmem_limit_bytes=vmem)


def _rms(x, gain):
    return x * lax.rsqrt(jnp.mean(x * x, axis=-1, keepdims=True) + EPS) * gain


def _norm_matmul_kernel(x_ref, g_ref, w_ref, o_ref, a_s):
    @pl.when(pl.program_id(1) == 0)
    def _():
        for c in range(0, a_s.shape[0], NORM_CHUNK):
            a_s[c:c + NORM_CHUNK, :] = _rms(x_ref[c:c + NORM_CHUNK, :], g_ref[...]).astype(BF16)

    o_ref[...] = jnp.dot(a_s[...], w_ref[...], preferred_element_type=F32).astype(o_ref.dtype)


def _norm_matmul(x, gain, w, layer, *, tm=1024, tn=1024):
    t, d = x.shape
    n = w.shape[2]
    return pl.pallas_call(
        _norm_matmul_kernel,
        grid=(t // tm, n // tn),
        in_specs=[
            pl.BlockSpec((tm, d), lambda i, j: (i, 0)),
            pl.BlockSpec((1, d), lambda i, j: (0, 0)),
            pl.BlockSpec((None, d, tn), lambda i, j: (layer, 0, j)),
        ],
        out_specs=pl.BlockSpec((tm, tn), lambda i, j: (i, j)),
        out_shape=jax.ShapeDtypeStruct((t, n), BF16),
        scratch_shapes=[pltpu.VMEM((tm, d), BF16)],
        compiler_params=_params("parallel", "arbitrary"),
        name="norm_matmul",
    )(x, gain.reshape(1, d), w)


def _matmul_resid_kernel(a_ref, w_ref, r_ref, o_ref):
    o_ref[...] = r_ref[...] + jnp.dot(a_ref[...], w_ref[...], preferred_element_type=F32)


def _matmul_resid(a, w, layer, resid, *, tm=512, tn=2048):
    t, k = a.shape
    n = w.shape[2]
    return pl.pallas_call(
        _matmul_resid_kernel,
        grid=(t // tm, n // tn),
        in_specs=[
            pl.BlockSpec((tm, k), lambda i, j: (i, 0)),
            pl.BlockSpec((None, k, tn), lambda i, j: (layer, 0, j)),
            pl.BlockSpec((tm, tn), lambda i, j: (i, j)),
        ],
        out_specs=pl.BlockSpec((tm, tn), lambda i, j: (i, j)),
        out_shape=jax.ShapeDtypeStruct((t, n), F32),
        compiler_params=_params("parallel", "arbitrary"),
        name="matmul_resid",
    )(a, w, resid)


def _ffn_kernel(h_ref, hp_ref, hn_ref, g_ref, wg_ref, wu_ref, cwg_ref, cwu_ref,
                cbg_ref, cbu_ref, wd_ref, fg_ref, *rest, tm, seq_tiles, final_norm, cast_next):
    if cast_next:
        next_up_ref, next_down_ref, o_ref, next_up_out, next_down_out, a_s = rest
        next_up_out[...] = next_up_ref[...].astype(BF16)
        next_down_out[...] = next_down_ref[...].astype(BF16)
    else:
        o_ref, a_s = rest
    i = pl.program_id(0)
    j = pl.program_id(1)
    rows = tm + HALO

    @pl.when(j == 0)
    def _():
        gain = g_ref[...]
        for c in range(0, tm, NORM_CHUNK):
            x = h_ref[c:c + NORM_CHUNK, :]
            o_ref[c:c + NORM_CHUNK, :] = x
            a_s[c:c + NORM_CHUNK, :] = _rms(x, gain).astype(BF16)
        first = (i % seq_tiles) == 0
        last = (i % seq_tiles) == seq_tiles - 1
        r = lax.broadcasted_iota(jnp.int32, hp_ref.shape, 0)
        halo = jnp.where((r == 0) & ~last, _rms(hn_ref[...], gain),
                         jnp.where((r == HALO - 1) & ~first, _rms(hp_ref[...], gain), 0.0))
        a_s[tm:rows, :] = halo.astype(BF16)

    a = a_s[...]

    def conv_branch(w_ref, cw_ref, cb_ref):
        up = jnp.dot(a, w_ref[...], preferred_element_type=F32)
        cw = cw_ref[...]
        c = (pltpu.roll(up, 1, 0) * cw[0:1] + up * cw[1:2]
             + pltpu.roll(up, rows - 1, 0) * cw[2:3])
        return c[0:tm] + cb_ref[...]

    g = conv_branch(wg_ref, cwg_ref, cbg_ref)
    u = conv_branch(wu_ref, cwu_ref, cbu_ref)
    half_g = 0.5 * g
    act = ((half_g + half_g * jnp.tanh(half_g)) * u).astype(BF16)
    o_ref[...] += jnp.dot(act, wd_ref[...], preferred_element_type=F32)

    if final_norm:
        @pl.when(j == pl.num_programs(1) - 1)
        def _():
            for c in range(0, tm, NORM_CHUNK):
                o_ref[c:c + NORM_CHUNK, :] = _rms(o_ref[c:c + NORM_CHUNK, :], fg_ref[...])


def _ffn(h, gain, w_up, conv_w, conv_b, w_down, layer, final_gain, w_up_f32, w_down_f32,
         *, final_norm, tm=1024, tn=512):
    t, d = h.shape
    d_ff = w_down.shape[0]
    depth = w_up_f32.shape[0]
    ni, nj = t // tm, d_ff // tn
    seq_tiles = SEQ // tm
    hb = tm // HALO
    n_hblocks = t // HALO
    cast_next = layer + 1 < depth
    in_specs = [
        pl.BlockSpec((tm, d), lambda i, j: (i, 0)),
        pl.BlockSpec((HALO, d), lambda i, j: (jnp.maximum(i * hb - 1, 0), 0)),
        pl.BlockSpec((HALO, d), lambda i, j: (jnp.minimum((i + 1) * hb, n_hblocks - 1), 0)),
        pl.BlockSpec((1, d), lambda i, j: (0, 0)),
        pl.BlockSpec((d, tn), lambda i, j: (0, j)),
        pl.BlockSpec((d, tn), lambda i, j: (0, j + nj)),
        pl.BlockSpec((None, 3, tn), lambda i, j: (layer, 0, j)),
        pl.BlockSpec((None, 3, tn), lambda i, j: (layer, 0, j + nj)),
        pl.BlockSpec((None, 1, tn), lambda i, j: (layer, 0, j)),
        pl.BlockSpec((None, 1, tn), lambda i, j: (layer, 0, j + nj)),
        pl.BlockSpec((tn, d), lambda i, j: (j, 0)),
        pl.BlockSpec((1, d), lambda i, j: (0, 0)),
    ]
    operands = [h, h, h, gain.reshape(1, d), w_up, w_up, conv_w, conv_w, conv_b, conv_b, w_down,
                final_gain.reshape(1, d)]
    out_specs = [pl.BlockSpec((tm, d), lambda i, j: (i, 0))]
    out_shape = [jax.ShapeDtypeStruct((t, d), F32)]
    if cast_next:
        up_blk, down_blk = (d // ni, 2 * d_ff // nj), (d_ff // nj, d // ni)
        in_specs += [pl.BlockSpec((None,) + up_blk, lambda i, j: (layer + 1, i, j)),
                     pl.BlockSpec((None,) + down_blk, lambda i, j: (layer + 1, j, i))]
        operands += [w_up_f32, w_down_f32]
        out_specs += [pl.BlockSpec(up_blk, lambda i, j: (i, j)),
                      pl.BlockSpec(down_blk, lambda i, j: (j, i))]
        out_shape += [jax.ShapeDtypeStruct(w_up.shape, BF16), jax.ShapeDtypeStruct(w_down.shape, BF16)]
    return pl.pallas_call(
        partial(_ffn_kernel, tm=tm, seq_tiles=seq_tiles, final_norm=final_norm, cast_next=cast_next),
        grid=(ni, nj),
        in_specs=in_specs,
        out_specs=out_specs,
        out_shape=out_shape,
        scratch_shapes=[pltpu.VMEM((tm + HALO, d), BF16)],
        compiler_params=_params("parallel", "arbitrary", vmem=60 * MIB),
        name="conv_glu_ffn",
    )(*operands)


def _na_bias_table(rpb):
    heads, _, n_dc = rpb.shape
    cols = jnp.arange(GRID_W)
    c_start = jnp.clip(cols - NA_WIN_C // 2, 0, GRID_W - NA_WIN_C)
    col_valid = (cols[None, :] >= c_start[:, None]) & (cols[None, :] < c_start[:, None] + NA_WIN_C)
    offs = cols[None, :] - cols[:, None] + NA_WIN_C - 1
    onehot = (offs[None] == jnp.arange(n_dc)[:, None, None]).astype(F32)
    tbl = jnp.einsum('hdj,jqk->hqdk', rpb * LOG2E, onehot, precision=lax.Precision.HIGHEST)
    tbl = jnp.where(col_valid[None, :, None, :], tbl, NEG_INF).reshape(heads, GRID_W, -1)
    left = NA_PAD_ROWS * GRID_W
    pad = lambda x, l: jnp.pad(x, ((0, 0), (0, 0), (l, NA_TABLE_LANES - l - x.shape[-1])),
                               constant_values=NEG_INF)
    return jnp.stack([pad(tbl, left), pad(tbl, left - GRID_W)], axis=1).astype(F32)


def _with_ones_column(v):
    return jnp.concatenate([v, jnp.ones_like(v)], axis=1)


def _na_kernel(q_ref, k_ref, v_ref, b_ref, o_ref, s_s, p_s, v_s, *, group):
    union_rows = NA_WIN_R + NA_BLOCK_ROWS - 1
    union = union_rows * GRID_W
    bq = NA_BLOCK_ROWS * GRID_W
    n_groups = GRID_ROWS // (NA_BLOCK_ROWS * group)
    lanes = [slice(hd * HEAD_DIM, (hd + 1) * HEAD_DIM) for hd in range(NA_HEADS_PER_STEP)]
    for hd, ls in enumerate(lanes):
        v_s[hd] = _with_ones_column(v_ref[:, ls])

    def r_start(r):
        return min(max(r - NA_WIN_R // 2, 0), GRID_ROWS - NA_WIN_R)

    def k_start(blk):
        return min(r_start(blk * NA_BLOCK_ROWS), GRID_ROWS - union_rows)

    def row_bias(hd, r, start):
        first = NA_WIN_R - 1 + NA_PAD_ROWS - (r - start)
        lane0 = (first // 2) * 2 * GRID_W
        b = b_ref[hd, first % 2, :, lane0:lane0 + union]
        lo, hi = (r_start(r) - start) * GRID_W, (r_start(r) - start + NA_WIN_R) * GRID_W
        tiles = []
        for t0 in range(0, union, LANES):
            t1 = min(t0 + LANES, union)
            tile = b[:, t0:t1]
            if t1 <= lo or t0 >= hi:
                tile = jnp.full_like(tile, NEG_INF)
            elif t0 < lo or t1 > hi:
                lane = t0 + lax.broadcasted_iota(jnp.int32, tile.shape, 1)
                tile = jnp.where((lane >= lo) & (lane < hi), tile, NEG_INF)
            tiles.append(tile)
        return jnp.concatenate(tiles, axis=1)

    def blocks(g):
        return range(g * group, (g + 1) * group)

    def scores(hd, g):
        for blk in blocks(g):
            q0, start = blk * bq, k_start(blk)
            q = (q_ref[q0:q0 + bq, lanes[hd]].astype(F32) * (HEAD_DIM ** -0.5 * LOG2E)).astype(BF16)
            s = lax.dot_general(q, k_ref[start * GRID_W:start * GRID_W + union, lanes[hd]],
                                (((1,), (1,)), ((), ())), preferred_element_type=F32)
            bias = jnp.concatenate([row_bias(hd, blk * NA_BLOCK_ROWS + i, start)
                                    for i in range(NA_BLOCK_ROWS)], axis=0)
            s_s[hd, q0:q0 + bq, :] = s + bias

    def softmax(hd, g):
        rs = slice(g * group * bq, (g + 1) * group * bq)
        s = s_s[hd, rs, :]
        p_s[hd, rs, :] = jnp.exp2(s - jnp.max(s, axis=-1, keepdims=True)).astype(BF16)

    def weighted_values(hd, g):
        for blk in blocks(g):
            q0, k0 = blk * bq, k_start(blk) * GRID_W
            o = jnp.dot(p_s[hd, q0:q0 + bq, :], v_s[hd, k0:k0 + union, :],
                        preferred_element_type=F32)
            o_ref[q0:q0 + bq, lanes[hd]] = (
                o[:, :HEAD_DIM] / o[:, HEAD_DIM:HEAD_DIM + 1]).astype(o_ref.dtype)

    stages = [(hd, g) for hd in range(NA_HEADS_PER_STEP) for g in range(n_groups)]
    scores(*stages[0])
    for n, stage in enumerate(stages):
        if n + 1 < len(stages):
            scores(*stages[n + 1])
        softmax(*stage)
        weighted_values(*stage)


def _na_attention(qkv, bias_tbl, batch, *, group=4):
    t = qkv.shape[0]
    union = (NA_WIN_R + NA_BLOCK_ROWS - 1) * GRID_W
    hps = NA_HEADS_PER_STEP
    n_steps = NA_HEADS // hps
    blk = lambda off: pl.BlockSpec((SEQ, hps * HEAD_DIM), lambda h, b: (b, h + off))
    return pl.pallas_call(
        partial(_na_kernel, group=group),
        grid=(n_steps, batch),
        in_specs=[
            blk(0), blk(n_steps), blk(2 * n_steps),
            pl.BlockSpec((hps, 2, GRID_W, NA_TABLE_LANES), lambda h, b: (h, 0, 0, 0)),
        ],
        out_specs=pl.BlockSpec((SEQ, hps * HEAD_DIM), lambda h, b: (b, h)),
        out_shape=jax.ShapeDtypeStruct((t, NA_HEADS * HEAD_DIM), BF16),
        scratch_shapes=[pltpu.VMEM((hps, SEQ, union), F32), pltpu.VMEM((hps, SEQ, union), BF16),
                        pltpu.VMEM((hps, SEQ, 2 * HEAD_DIM), BF16)],
        compiler_params=_params("parallel", "arbitrary"),
        name="neighborhood_attention",
    )(qkv, qkv, qkv, bias_tbl)


def _sc_out_kernel(gb_ref, gc_ref, hh_ref, gcp_ref, hhp_ref, gcn_ref, hhn_ref, cw_ref,
                   w_ref, r_ref, o_ref, *, tm, seq_tiles, chunk):
    i = pl.program_id(0)
    first = (i % seq_tiles) == 0
    last = (i % seq_tiles) == seq_tiles - 1
    rows = lax.broadcasted_iota(jnp.int32, (tm, chunk), 0)
    acc = r_ref[...]
    for c in range(0, w_ref.shape[0], chunk):
        cs = slice(c, c + chunk)
        p = gc_ref[:, cs].astype(F32) * hh_ref[:, cs].astype(F32)
        prev = gcp_ref[HALO - 1:HALO, cs].astype(F32) * hhp_ref[HALO - 1:HALO, cs].astype(F32)
        nxt = gcn_ref[0:1, cs].astype(F32) * hhn_ref[0:1, cs].astype(F32)
        prev = jnp.where(first, 0.0, prev)
        nxt = jnp.where(last, 0.0, nxt)
        pm1 = jnp.where(rows == 0, prev, pltpu.roll(p, 1, 0))
        pp1 = jnp.where(rows == tm - 1, nxt, pltpu.roll(p, tm - 1, 0))
        cw = cw_ref[:, cs]
        conv = pm1 * cw[0:1] + p * cw[1:2] + pp1 * cw[2:3]
        y = (gb_ref[:, cs].astype(F32) * conv).astype(BF16)
        acc = acc + jnp.dot(y, w_ref[cs, :], preferred_element_type=F32)
    o_ref[...] = acc


def _sc_out(z, conv_w, w_out, layer, resid, *, tm=512, chunk=512):
    t = z.shape[0]
    _, d, n = w_out.shape
    seq_tiles = SEQ // tm
    hb = tm // HALO
    n_hblocks = t // HALO
    main = lambda c: pl.BlockSpec((tm, d), lambda i: (i, c))
    prev = lambda c: pl.BlockSpec((HALO, d), lambda i: (jnp.maximum(i * hb - 1, 0), c))
    nxt = lambda c: pl.BlockSpec((HALO, d), lambda i: (jnp.minimum((i + 1) * hb, n_hblocks - 1), c))
    return pl.pallas_call(
        partial(_sc_out_kernel, tm=tm, seq_tiles=seq_tiles, chunk=chunk),
        grid=(t // tm,),
        in_specs=[
            main(0), main(1), main(2), prev(1), prev(2), nxt(1), nxt(2),
            pl.BlockSpec((None, 3, d), lambda i: (layer, 0, 0)),
            pl.BlockSpec((None, d, n), lambda i: (layer, 0, 0)),
            pl.BlockSpec((tm, n), lambda i: (i, 0)),
        ],
        out_specs=pl.BlockSpec((tm, n), lambda i: (i, 0)),
        out_shape=jax.ShapeDtypeStruct((t, n), F32),
        compiler_params=_params("parallel"),
        name="short_conv_out",
    )(z, z, z, z, z, z, z, conv_w, w_out, resid)


def _rope_tables():
    t = jnp.arange(SEQ)
    row = (t // GRID_W).astype(F32)[:, None]
    col = (t % GRID_W).astype(F32)[:, None]
    half = HEAD_DIM // 2
    inv = ROPE_THETA ** (-jnp.arange(0, half, 2, dtype=F32) / half)
    ang = jnp.concatenate([row * inv, row * inv, col * inv, col * inv], axis=-1)
    cos, sin = jnp.cos(ang), jnp.sin(ang)
    lane = jnp.arange(HEAD_DIM)[None, :]
    lower = (lane % half) < (half // 2)
    sin_lo = jnp.where(lower, -sin, 0.0)
    sin_hi = jnp.where(lower, 0.0, sin)
    return cos, sin_lo, sin_hi


def _norm_rope(x, gain, cos, sin_lo, sin_hi):
    y = _rms(x, gain)
    quarter = HEAD_DIM // 4
    return (y * cos + pltpu.roll(y, HEAD_DIM - quarter, 1) * sin_lo
            + pltpu.roll(y, quarter, 1) * sin_hi)


def _gqa_kernel(q_ref, k_ref, v_ref, qn_ref, kn_ref, cq_ref, slq_ref, shq_ref,
                ck_ref, slk_ref, shk_ref, o_ref, kt_s, v_s, *, row_parts):
    @pl.when(pl.program_id(2) == 0)
    def _():
        k = _norm_rope(k_ref[...].astype(F32), kn_ref[...], ck_ref[...], slk_ref[...], shk_ref[...])
        kt_s[...] = k.T.astype(BF16)
        v_s[...] = _with_ones_column(v_ref[...])

    cos, sin_lo, sin_hi = cq_ref[...], slq_ref[...], shq_ref[...]
    tq = q_ref.shape[0]
    units = [(slice(r, r + tq // row_parts), slice(g * HEAD_DIM, (g + 1) * HEAD_DIM))
             for g in range(GQA_GROUP) for r in range(0, tq, tq // row_parts)]

    def scores(rs, hs):
        q = _norm_rope(q_ref[rs, hs].astype(F32), qn_ref[...], cos[rs], sin_lo[rs], sin_hi[rs])
        q = (q * (HEAD_DIM ** -0.5 * LOG2E)).astype(BF16)
        return jnp.dot(q, kt_s[...], preferred_element_type=F32)

    def weighted_values(rs, hs, s):
        p = jnp.exp2(s - jnp.max(s, axis=-1, keepdims=True)).astype(BF16)
        o = jnp.dot(p, v_s[...], preferred_element_type=F32)
        o_ref[rs, hs] = (o[:, :HEAD_DIM] / o[:, HEAD_DIM:HEAD_DIM + 1]).astype(o_ref.dtype)

    s_next = scores(*units[0])
    for n, unit in enumerate(units):
        s = s_next
        if n + 1 < len(units):
            s_next = scores(*units[n + 1])
        weighted_values(*unit, s)


def _gqa_attention(qkv, q_norm, k_norm, batch, *, tq=1024, row_parts=4):
    t = qkv.shape[0]
    nq = SEQ // tq
    gw = GQA_GROUP * HEAD_DIM
    cos, sin_lo, sin_hi = _rope_tables()
    qtab = pl.BlockSpec((tq, HEAD_DIM), lambda b, h, i: (i, 0))
    ktab = pl.BlockSpec((SEQ, HEAD_DIM), lambda b, h, i: (0, 0))
    vec = pl.BlockSpec((1, HEAD_DIM), lambda b, h, i: (0, 0))
    return pl.pallas_call(
        partial(_gqa_kernel, row_parts=row_parts),
        grid=(batch, GQA_KV_HEADS, nq),
        in_specs=[
            pl.BlockSpec((tq, gw), lambda b, h, i: (b * nq + i, h)),
            pl.BlockSpec((SEQ, HEAD_DIM), lambda b, h, i: (b, GQA_Q_HEADS + h)),
            pl.BlockSpec((SEQ, HEAD_DIM), lambda b, h, i: (b, GQA_Q_HEADS + GQA_KV_HEADS + h)),
            vec, vec, qtab, qtab, qtab, ktab, ktab, ktab,
        ],
        out_specs=pl.BlockSpec((tq, gw), lambda b, h, i: (b * nq + i, h)),
        out_shape=jax.ShapeDtypeStruct((t, GQA_Q_HEADS * HEAD_DIM), BF16),
        scratch_shapes=[pltpu.VMEM((HEAD_DIM, SEQ), BF16), pltpu.VMEM((SEQ, 2 * HEAD_DIM), BF16)],
        compiler_params=_params("parallel", "parallel", "arbitrary"),
        name="gqa_attention",
    )(qkv, qkv, qkv, q_norm.reshape(1, -1), k_norm.reshape(1, -1),
      cos, sin_lo, sin_hi, cos, sin_lo, sin_hi)


def kernel(x, mix_norm, ffn_norm, final_norm, na_w_qkv, na_rpb, na_w_o, sc_w_in, sc_conv_w,
           sc_w_out, gqa_w_qkv, gqa_q_norm, gqa_k_norm, gqa_w_o, ffn_w_up, ffn_conv_w,
           ffn_conv_b, ffn_w_down):
    batch, seq, d = x.shape
    assert (seq, d) == (SEQ, D_MODEL)
    depth = mix_norm.shape[0]
    h = x.reshape(batch * seq, d)
    bf = lambda w: w.astype(BF16)
    na_w_qkv, na_w_o, sc_w_in, sc_w_out = bf(na_w_qkv), bf(na_w_o), bf(sc_w_in), bf(sc_w_out)
    gqa_w_qkv, gqa_w_o = bf(gqa_w_qkv), bf(gqa_w_o)
    w_up, w_down = bf(ffn_w_up[0]), bf(ffn_w_down[0])
    ffn_conv_b = ffn_conv_b.reshape(depth, 1, -1)
    for i in range(depth):
        m, j = i % N_MIXERS, i // N_MIXERS
        if m == 0:
            qkv = _norm_matmul(h, mix_norm[i], na_w_qkv, j)
            att = _na_attention(qkv, _na_bias_table(na_rpb[j]), batch)
            h = _matmul_resid(att, na_w_o, j, h)
        elif m == 1:
            z = _norm_matmul(h, mix_norm[i], sc_w_in, j)
            h = _sc_out(z, sc_conv_w, sc_w_out, j, h)
        else:
            qkv = _norm_matmul(h, mix_norm[i], gqa_w_qkv, j)
            att = _gqa_attention(qkv, gqa_q_norm[j], gqa_k_norm[j], batch)
            h = _matmul_resid(att, gqa_w_o, j, h)
        outs = _ffn(h, ffn_norm[i], w_up, ffn_conv_w, ffn_conv_b, w_down, i, final_norm,
                    ffn_w_up, ffn_w_down, final_norm=(i == depth - 1))
        h = outs[0]
        if i + 1 < depth:
            _, w_up, w_down = outs
    return h.reshape(batch, seq, d)
```

```python
from functools import partial

import jax
import jax.numpy as jnp
from jax import lax
from jax.experimental import pallas as pl
from jax.experimental.pallas import tpu as pltpu

D_MODEL = 2048
SEQ = 2048
GRID_W = 64
GRID_ROWS = SEQ // GRID_W
HEAD_DIM = 128
EPS = 1e-6
NEG_INF = -1e30
NA_HEADS = D_MODEL // HEAD_DIM
NA_WIN_R = 8
NA_WIN_C = 16
NA_BLOCK_ROWS = 2
NA_PAD_ROWS = NA_BLOCK_ROWS - 1
NA_HEADS_PER_STEP = 2
GQA_Q_HEADS = D_MODEL // HEAD_DIM
GQA_KV_HEADS = GQA_Q_HEADS // 4
GQA_GROUP = GQA_Q_HEADS // GQA_KV_HEADS
ROPE_THETA = 10000.0
N_MIXERS = 3

HALO = 16
LANES = 128
NA_TABLE_LANES = -(-(2 * NA_WIN_R - 1 + 2 * NA_PAD_ROWS) * GRID_W // LANES) * LANES
NORM_CHUNK = 64
LOG2E = 1.4426950408889634
MIB = 1024 * 1024
VMEM_LIMIT = 48 * MIB

BF16 = jnp.bfloat16
F32 = jnp.float32


def _params(*sem, vmem=VMEM_LIMIT):
    return pltpu.CompilerParams(dimension_semantics=sem, vmem_limit_bytes=vmem)


def _rms(x, gain):
    return x * lax.rsqrt(jnp.mean(x * x, axis=-1, keepdims=True) + EPS) * gain


def _norm_matmul_kernel(x_ref, g_ref, w_ref, o_ref, a_s):
    @pl.when(pl.program_id(1) == 0)
    def _():
        for c in range(0, a_s.shape[0], NORM_CHUNK):
            a_s[c:c + NORM_CHUNK, :] = _rms(x_ref[c:c + NORM_CHUNK, :], g_ref[...]).astype(BF16)

    o_ref[...] = jnp.dot(a_s[...], w_ref[...], preferred_element_type=F32).astype(o_ref.dtype)


def _norm_matmul(x, gain, w, layer, *, tm=1024, tn=1024):
    t, d = x.shape
    n = w.shape[2]
    return pl.pallas_call(
        _norm_matmul_kernel,
        grid=(t // tm, n // tn),
        in_specs=[
            pl.BlockSpec((tm, d), lambda i, j: (i, 0)),
            pl.BlockSpec((1, d), lambda i, j: (0, 0)),
            pl.BlockSpec((None, d, tn), lambda i, j: (layer, 0, j)),
        ],
        out_specs=pl.BlockSpec((tm, tn), lambda i, j: (i, j)),
        out_shape=jax.ShapeDtypeStruct((t, n), BF16),
        scratch_shapes=[pltpu.VMEM((tm, d), BF16)],
        compiler_params=_params("parallel", "arbitrary"),
        name="norm_matmul",
    )(x, gain.reshape(1, d), w)


def _matmul_resid_kernel(a_ref, w_ref, r_ref, o_ref):
    o_ref[...] = r_ref[...] + jnp.dot(a_ref[...], w_ref[...], preferred_element_type=F32)


def _matmul_resid(a, w, layer, resid, *, tm=512, tn=2048):
    t, k = a.shape
    n = w.shape[2]
    return pl.pallas_call(
        _matmul_resid_kernel,
        grid=(t // tm, n // tn),
        in_specs=[
            pl.BlockSpec((tm, k), lambda i, j: (i, 0)),
            pl.BlockSpec((None, k, tn), lambda i, j: (layer, 0, j)),
            pl.BlockSpec((tm, tn), lambda i, j: (i, j)),
        ],
        out_specs=pl.BlockSpec((tm, tn), lambda i, j: (i, j)),
        out_shape=jax.ShapeDtypeStruct((t, n), F32),
        compiler_params=_params("parallel", "arbitrary"),
        name="matmul_resid",
    )(a, w, resid)


def _ffn_kernel(h_ref, hp_ref, hn_ref, g_ref, wg_ref, wu_ref, cwg_ref, cwu_ref,
                cbg_ref, cbu_ref, wd_ref, fg_ref, *rest, tm, nj, seq_tiles, final_norm, cast_steps):
    n_cast = len(cast_steps)
    cast_in, o_ref, cast_out, a_s = rest[:n_cast], rest[n_cast], rest[n_cast + 1:-1], rest[-1]
    i = pl.program_id(0)
    j = pl.program_id(1)
    rows = tm + HALO

    for src, dst, steps in zip(cast_in, cast_out, cast_steps):
        if steps == nj:
            dst[...] = src[...].astype(BF16)
        else:
            @pl.when(j < steps)
            def _(src=src, dst=dst):
                dst[...] = src[...].astype(BF16)

    @pl.when(j == 0)
    def _():
        gain = g_ref[...]
        for c in range(0, tm, NORM_CHUNK):
            x = h_ref[c:c + NORM_CHUNK, :]
            o_ref[c:c + NORM_CHUNK, :] = x
            a_s[c:c + NORM_CHUNK, :] = _rms(x, gain).astype(BF16)
        first = (i % seq_tiles) == 0
        last = (i % seq_tiles) == seq_tiles - 1
        r = lax.broadcasted_iota(jnp.int32, hp_ref.shape, 0)
        halo = jnp.where((r == 0) & ~last, _rms(hn_ref[...], gain),
                         jnp.where((r == HALO - 1) & ~first, _rms(hp_ref[...], gain), 0.0))
        a_s[tm:rows, :] = halo.astype(BF16)

    a = a_s[...]

    def conv_branch(w_ref, cw_ref, cb_ref):
        up = jnp.dot(a, w_ref[...], preferred_element_type=F32)
        cw = cw_ref[...]
        c = (pltpu.roll(up, 1, 0) * cw[0:1] + up * cw[1:2]
             + pltpu.roll(up, rows - 1, 0) * cw[2:3])
        return c[0:tm] + cb_ref[...]

    g = conv_branch(wg_ref, cwg_ref, cbg_ref)
    u = conv_branch(wu_ref, cwu_ref, cbu_ref)
    half_g = 0.5 * g
    act = ((half_g + half_g * jnp.tanh(half_g)) * u).astype(BF16)
    o_ref[...] += jnp.dot(act, wd_ref[...], preferred_element_type=F32)

    if final_norm:
        @pl.when(j == pl.num_programs(1) - 1)
        def _():
            for c in range(0, tm, NORM_CHUNK):
                o_ref[c:c + NORM_CHUNK, :] = _rms(o_ref[c:c + NORM_CHUNK, :], fg_ref[...])


def _ffn(h, gain, w_up, conv_w, conv_b, w_down, layer, final_gain, side_casts,
         *, final_norm, tm=1024, tn=512):
    t, d = h.shape
    d_ff = w_down.shape[0]
    ni, nj = t // tm, d_ff // tn
    seq_tiles = SEQ // tm
    hb = tm // HALO
    n_hblocks = t // HALO
    in_specs = [
        pl.BlockSpec((tm, d), lambda i, j: (i, 0)),
        pl.BlockSpec((HALO, d), lambda i, j: (jnp.maximum(i * hb - 1, 0), 0)),
        pl.BlockSpec((HALO, d), lambda i, j: (jnp.minimum((i + 1) * hb, n_hblocks - 1), 0)),
        pl.BlockSpec((1, d), lambda i, j: (0, 0)),
        pl.BlockSpec((d, tn), lambda i, j: (0, j)),
        pl.BlockSpec((d, tn), lambda i, j: (0, j + nj)),
        pl.BlockSpec((None, 3, tn), lambda i, j: (layer, 0, j)),
        pl.BlockSpec((None, 3, tn), lambda i, j: (layer, 0, j + nj)),
        pl.BlockSpec((None, 1, tn), lambda i, j: (layer, 0, j)),
        pl.BlockSpec((None, 1, tn), lambda i, j: (layer, 0, j + nj)),
        pl.BlockSpec((tn, d), lambda i, j: (j, 0)),
        pl.BlockSpec((1, d), lambda i, j: (0, 0)),
    ]
    operands = [h, h, h, gain.reshape(1, d), w_up, w_up, conv_w, conv_w, conv_b, conv_b, w_down,
                final_gain.reshape(1, d)]
    out_specs = [pl.BlockSpec((tm, d), lambda i, j: (i, 0))]
    out_shape = [jax.ShapeDtypeStruct((t, d), F32)]
    cast_specs, cast_steps = [], []
    for w, w_layer in side_casts:
        _, r, c = w.shape
        steps = max(s for s in range(1, nj + 1) if (c // LANES) % s == 0)
        blk = (r // ni, c // steps)
        cast_specs.append(pl.BlockSpec(
            (None,) + blk, lambda i, j, w_layer=w_layer, steps=steps: (w_layer, i, jnp.minimum(j, steps - 1))))
        out_specs.append(pl.BlockSpec(blk, lambda i, j, steps=steps: (i, jnp.minimum(j, steps - 1))))
        out_shape.append(jax.ShapeDtypeStruct((r, c), BF16))
        operands.append(w)
        cast_steps.append(steps)
    return pl.pallas_call(
        partial(_ffn_kernel, tm=tm, nj=nj, seq_tiles=seq_tiles, final_norm=final_norm,
                cast_steps=tuple(cast_steps)),
        grid=(ni, nj),
        in_specs=in_specs + cast_specs,
        out_specs=out_specs,
        out_shape=out_shape,
        scratch_shapes=[pltpu.VMEM((tm + HALO, d), BF16)],
        compiler_params=_params("parallel", "arbitrary", vmem=60 * MIB),
        name="conv_glu_ffn",
    )(*operands)


def _na_bias_table(rpb):
    heads, _, n_dc = rpb.shape
    cols = jnp.arange(GRID_W)
    c_start = jnp.clip(cols - NA_WIN_C // 2, 0, GRID_W - NA_WIN_C)
    col_valid = (cols[None, :] >= c_start[:, None]) & (cols[None, :] < c_start[:, None] + NA_WIN_C)
    offs = cols[None, :] - cols[:, None] + NA_WIN_C - 1
    onehot = (offs[None] == jnp.arange(n_dc)[:, None, None]).astype(F32)
    tbl = jnp.einsum('hdj,jqk->hqdk', rpb * LOG2E, onehot, precision=lax.Precision.HIGHEST)
    tbl = jnp.where(col_valid[None, :, None, :], tbl, NEG_INF).reshape(heads, GRID_W, -1)
    left = NA_PAD_ROWS * GRID_W
    pad = lambda x, l: jnp.pad(x, ((0, 0), (0, 0), (l, NA_TABLE_LANES - l - x.shape[-1])),
                               constant_values=NEG_INF)
    return jnp.stack([pad(tbl, left), pad(tbl, left - GRID_W)], axis=1).astype(F32)


def _with_ones_column(v):
    return jnp.concatenate([v, jnp.ones_like(v)], axis=1)


def _na_kernel(q_ref, k_ref, v_ref, b_ref, o_ref, s_s, p_s, v_s, *, group):
    union_rows = NA_WIN_R + NA_BLOCK_ROWS - 1
    union = union_rows * GRID_W
    bq = NA_BLOCK_ROWS * GRID_W
    n_groups = GRID_ROWS // (NA_BLOCK_ROWS * group)
    lanes = [slice(hd * HEAD_DIM, (hd + 1) * HEAD_DIM) for hd in range(NA_HEADS_PER_STEP)]
    for hd, ls in enumerate(lanes):
        v_s[hd] = _with_ones_column(v_ref[:, ls])

    def r_start(r):
        return min(max(r - NA_WIN_R // 2, 0), GRID_ROWS - NA_WIN_R)

    def k_start(blk):
        return min(r_start(blk * NA_BLOCK_ROWS), GRID_ROWS - union_rows)

    def row_bias(hd, r, start):
        first = NA_WIN_R - 1 + NA_PAD_ROWS - (r - start)
        lane0 = (first // 2) * 2 * GRID_W
        b = b_ref[hd, first % 2, :, lane0:lane0 + union]
        lo, hi = (r_start(r) - start) * GRID_W, (r_start(r) - start + NA_WIN_R) * GRID_W
        tiles = []
        for t0 in range(0, union, LANES):
            t1 = min(t0 + LANES, union)
            tile = b[:, t0:t1]
            if t1 <= lo or t0 >= hi:
                tile = jnp.full_like(tile, NEG_INF)
            elif t0 < lo or t1 > hi:
                lane = t0 + lax.broadcasted_iota(jnp.int32, tile.shape, 1)
                tile = jnp.where((lane >= lo) & (lane < hi), tile, NEG_INF)
            tiles.append(tile)
        return jnp.concatenate(tiles, axis=1)

    def blocks(g):
        return range(g * group, (g + 1) * group)

    def scores(hd, g):
        for blk in blocks(g):
            q0, start = blk * bq, k_start(blk)
            q = (q_ref[q0:q0 + bq, lanes[hd]].astype(F32) * (HEAD_DIM ** -0.5 * LOG2E)).astype(BF16)
            s = lax.dot_general(q, k_ref[start * GRID_W:start * GRID_W + union, lanes[hd]],
                                (((1,), (1,)), ((), ())), preferred_element_type=F32)
            bias = jnp.concatenate([row_bias(hd, blk * NA_BLOCK_ROWS + i, start)
                                    for i in range(NA_BLOCK_ROWS)], axis=0)
            s_s[hd, q0:q0 + bq, :] = s + bias

    def softmax(hd, g):
        rs = slice(g * group * bq, (g + 1) * group * bq)
        s = s_s[hd, rs, :]
        p_s[hd, rs, :] = jnp.exp2(s - jnp.max(s, axis=-1, keepdims=True)).astype(BF16)

    def weighted_values(hd, g):
        for blk in blocks(g):
            q0, k0 = blk * bq, k_start(blk) * GRID_W
            o = jnp.dot(p_s[hd, q0:q0 + bq, :], v_s[hd, k0:k0 + union, :],
                        preferred_element_type=F32)
            o_ref[q0:q0 + bq, lanes[hd]] = (
                o[:, :HEAD_DIM] / o[:, HEAD_DIM:HEAD_DIM + 1]).astype(o_ref.dtype)

    stages = [(hd, g) for hd in range(NA_HEADS_PER_STEP) for g in range(n_groups)]
    scores(*stages[0])
    for n, stage in enumerate(stages):
        if n + 1 < len(stages):
            scores(*stages[n + 1])
        softmax(*stage)
        weighted_values(*stage)


def _na_attention(qkv, bias_tbl, batch, *, group=4):
    t = qkv.shape[0]
    union = (NA_WIN_R + NA_BLOCK_ROWS - 1) * GRID_W
    hps = NA_HEADS_PER_STEP
    n_steps = NA_HEADS // hps
    blk = lambda off: pl.BlockSpec((SEQ, hps * HEAD_DIM), lambda h, b: (b, h + off))
    return pl.pallas_call(
        partial(_na_kernel, group=group),
        grid=(n_steps, batch),
        in_specs=[
            blk(0), blk(n_steps), blk(2 * n_steps),
            pl.BlockSpec((hps, 2, GRID_W, NA_TABLE_LANES), lambda h, b: (h, 0, 0, 0)),
        ],
        out_specs=pl.BlockSpec((SEQ, hps * HEAD_DIM), lambda h, b: (b, h)),
        out_shape=jax.ShapeDtypeStruct((t, NA_HEADS * HEAD_DIM), BF16),
        scratch_shapes=[pltpu.VMEM((hps, SEQ, union), F32), pltpu.VMEM((hps, SEQ, union), BF16),
                        pltpu.VMEM((hps, SEQ, 2 * HEAD_DIM), BF16)],
        compiler_params=_params("parallel", "arbitrary"),
        name="neighborhood_attention",
    )(qkv, qkv, qkv, bias_tbl)


def _sc_out_kernel(gb_ref, gc_ref, hh_ref, gcp_ref, hhp_ref, gcn_ref, hhn_ref, cw_ref,
                   w_ref, r_ref, o_ref, *, tm, seq_tiles, chunk):
    i = pl.program_id(0)
    first = (i % seq_tiles) == 0
    last = (i % seq_tiles) == seq_tiles - 1
    rows = lax.broadcasted_iota(jnp.int32, (tm, chunk), 0)
    acc = r_ref[...]
    for c in range(0, w_ref.shape[0], chunk):
        cs = slice(c, c + chunk)
        p = gc_ref[:, cs].astype(F32) * hh_ref[:, cs].astype(F32)
        prev = gcp_ref[HALO - 1:HALO, cs].astype(F32) * hhp_ref[HALO - 1:HALO, cs].astype(F32)
        nxt = gcn_ref[0:1, cs].astype(F32) * hhn_ref[0:1, cs].astype(F32)
        prev = jnp.where(first, 0.0, prev)
        nxt = jnp.where(last, 0.0, nxt)
        pm1 = jnp.where(rows == 0, prev, pltpu.roll(p, 1, 0))
        pp1 = jnp.where(rows == tm - 1, nxt, pltpu.roll(p, tm - 1, 0))
        cw = cw_ref[:, cs]
        conv = pm1 * cw[0:1] + p * cw[1:2] + pp1 * cw[2:3]
        y = (gb_ref[:, cs].astype(F32) * conv).astype(BF16)
        acc = acc + jnp.dot(y, w_ref[cs, :], preferred_element_type=F32)
    o_ref[...] = acc


def _sc_out(z, conv_w, w_out, layer, resid, *, tm=512, chunk=512):
    t = z.shape[0]
    _, d, n = w_out.shape
    seq_tiles = SEQ // tm
    hb = tm // HALO
    n_hblocks = t // HALO
    main = lambda c: pl.BlockSpec((tm, d), lambda i: (i, c))
    prev = lambda c: pl.BlockSpec((HALO, d), lambda i: (jnp.maximum(i * hb - 1, 0), c))
    nxt = lambda c: pl.BlockSpec((HALO, d), lambda i: (jnp.minimum((i + 1) * hb, n_hblocks - 1), c))
    return pl.pallas_call(
        partial(_sc_out_kernel, tm=tm, seq_tiles=seq_tiles, chunk=chunk),
        grid=(t // tm,),
        in_specs=[
            main(0), main(1), main(2), prev(1), prev(2), nxt(1), nxt(2),
            pl.BlockSpec((None, 3, d), lambda i: (layer, 0, 0)),
            pl.BlockSpec((None, d, n), lambda i: (layer, 0, 0)),
            pl.BlockSpec((tm, n), lambda i: (i, 0)),
        ],
        out_specs=pl.BlockSpec((tm, n), lambda i: (i, 0)),
        out_shape=jax.ShapeDtypeStruct((t, n), F32),
        compiler_params=_params("parallel"),
        name="short_conv_out",
    )(z, z, z, z, z, z, z, conv_w, w_out, resid)


def _rope_tables():
    t = jnp.arange(SEQ)
    row = (t // GRID_W).astype(F32)[:, None]
    col = (t % GRID_W).astype(F32)[:, None]
    half = HEAD_DIM // 2
    inv = ROPE_THETA ** (-jnp.arange(0, half, 2, dtype=F32) / half)
    ang = jnp.concatenate([row * inv, row * inv, col * inv, col * inv], axis=-1)
    cos, sin = jnp.cos(ang), jnp.sin(ang)
    lane = jnp.arange(HEAD_DIM)[None, :]
    lower = (lane % half) < (half // 2)
    sin_lo = jnp.where(lower, -sin, 0.0)
    sin_hi = jnp.where(lower, 0.0, sin)
    return cos, sin_lo, sin_hi


def _norm_rope(x, gain, cos, sin_lo, sin_hi):
    y = _rms(x, gain)
    quarter = HEAD_DIM // 4
    return (y * cos + pltpu.roll(y, HEAD_DIM - quarter, 1) * sin_lo
            + pltpu.roll(y, quarter, 1) * sin_hi)


def _gqa_kernel(q_ref, k_ref, v_ref, qn_ref, kn_ref, cq_ref, slq_ref, shq_ref,
                ck_ref, slk_ref, shk_ref, o_ref, kt_s, v_s, *, row_parts):
    @pl.when(pl.program_id(2) == 0)
    def _():
        k = _norm_rope(k_ref[...].astype(F32), kn_ref[...], ck_ref[...], slk_ref[...], shk_ref[...])
        kt_s[...] = k.T.astype(BF16)
        v_s[...] = _with_ones_column(v_ref[...])

    cos, sin_lo, sin_hi = cq_ref[...], slq_ref[...], shq_ref[...]
    tq = q_ref.shape[0]
    units = [(slice(r, r + tq // row_parts), slice(g * HEAD_DIM, (g + 1) * HEAD_DIM))
             for g in range(GQA_GROUP) for r in range(0, tq, tq // row_parts)]

    def scores(rs, hs):
        q = _norm_rope(q_ref[rs, hs].astype(F32), qn_ref[...], cos[rs], sin_lo[rs], sin_hi[rs])
        q = (q * (HEAD_DIM ** -0.5 * LOG2E)).astype(BF16)
        return jnp.dot(q, kt_s[...], preferred_element_type=F32)

    def weighted_values(rs, hs, s):
        p = jnp.exp2(s - jnp.max(s, axis=-1, keepdims=True)).astype(BF16)
        o = jnp.dot(p, v_s[...], preferred_element_type=F32)
        o_ref[rs, hs] = (o[:, :HEAD_DIM] / o[:, HEAD_DIM:HEAD_DIM + 1]).astype(o_ref.dtype)

    s_next = scores(*units[0])
    for n, unit in enumerate(units):
        s = s_next
        if n + 1 < len(units):
            s_next = scores(*units[n + 1])
        weighted_values(*unit, s)


def _gqa_attention(qkv, q_norm, k_norm, batch, *, tq=1024, row_parts=4):
    t = qkv.shape[0]
    nq = SEQ // tq
    gw = GQA_GROUP * HEAD_DIM
    cos, sin_lo, sin_hi = _rope_tables()
    qtab = pl.BlockSpec((tq, HEAD_DIM), lambda b, h, i: (i, 0))
    ktab = pl.BlockSpec((SEQ, HEAD_DIM), lambda b, h, i: (0, 0))
    vec = pl.BlockSpec((1, HEAD_DIM), lambda b, h, i: (0, 0))
    return pl.pallas_call(
        partial(_gqa_kernel, row_parts=row_parts),
        grid=(batch, GQA_KV_HEADS, nq),
        in_specs=[
            pl.BlockSpec((tq, gw), lambda b, h, i: (b * nq + i, h)),
            pl.BlockSpec((SEQ, HEAD_DIM), lambda b, h, i: (b, GQA_Q_HEADS + h)),
            pl.BlockSpec((SEQ, HEAD_DIM), lambda b, h, i: (b, GQA_Q_HEADS + GQA_KV_HEADS + h)),
            vec, vec, qtab, qtab, qtab, ktab, ktab, ktab,
        ],
        out_specs=pl.BlockSpec((tq, gw), lambda b, h, i: (b * nq + i, h)),
        out_shape=jax.ShapeDtypeStruct((t, GQA_Q_HEADS * HEAD_DIM), BF16),
        scratch_shapes=[pltpu.VMEM((HEAD_DIM, SEQ), BF16), pltpu.VMEM((SEQ, 2 * HEAD_DIM), BF16)],
        compiler_params=_params("parallel", "parallel", "arbitrary"),
        name="gqa_attention",
    )(qkv, qkv, qkv, q_norm.reshape(1, -1), k_norm.reshape(1, -1),
      cos, sin_lo, sin_hi, cos, sin_lo, sin_hi)


def kernel(x, mix_norm, ffn_norm, final_norm, na_w_qkv, na_rpb, na_w_o, sc_w_in, sc_conv_w,
           sc_w_out, gqa_w_qkv, gqa_q_norm, gqa_k_norm, gqa_w_o, ffn_w_up, ffn_conv_w,
           ffn_conv_b, ffn_w_down):
    batch, seq, d = x.shape
    assert (seq, d) == (SEQ, D_MODEL)
    depth = mix_norm.shape[0]
    h = x.reshape(batch * seq, d)
    mixer_f32 = ((na_w_qkv, na_w_o), (sc_w_in, sc_w_out), (gqa_w_qkv, gqa_w_o))

    def layer_weights_f32(i):
        return [(w, i // N_MIXERS) for w in mixer_f32[i % N_MIXERS]] + [(ffn_w_up, i), (ffn_w_down, i)]

    w_in, w_out, w_up, w_down = [w[l].astype(BF16) for w, l in layer_weights_f32(0)]
    ffn_conv_b = ffn_conv_b.reshape(depth, 1, -1)
    for i in range(depth):
        m, j = i % N_MIXERS, i // N_MIXERS
        w_in, w_out = w_in[None], w_out[None]
        if m == 0:
            qkv = _norm_matmul(h, mix_norm[i], w_in, 0)
            att = _na_attention(qkv, _na_bias_table(na_rpb[j]), batch)
            h = _matmul_resid(att, w_out, 0, h)
        elif m == 1:
            z = _norm_matmul(h, mix_norm[i], w_in, 0)
            h = _sc_out(z, sc_conv_w[j][None], w_out, 0, h)
        else:
            qkv = _norm_matmul(h, mix_norm[i], w_in, 0)
            att = _gqa_attention(qkv, gqa_q_norm[j], gqa_k_norm[j], batch)
            h = _matmul_resid(att, w_out, 0, h)
        side_casts = layer_weights_f32(i + 1) if i + 1 < depth else []
        h, *next_weights = _ffn(h, ffn_norm[i], w_up, ffn_conv_w, ffn_conv_b, w_down, i, final_norm,
                                side_casts, final_norm=(i == depth - 1))
        if next_weights:
            w_in, w_out, w_up, w_down = next_weights
    return h.reshape(batch, seq, d)
```

```python
from functools import partial

import jax
import jax.numpy as jnp
from jax import lax
from jax.experimental import pallas as pl
from jax.experimental.pallas import tpu as pltpu

D_MODEL = 2048
SEQ = 2048
GRID_W = 64
GRID_ROWS = SEQ // GRID_W
HEAD_DIM = 128
EPS = 1e-6
NEG_INF = -1e30
NA_HEADS = D_MODEL // HEAD_DIM
NA_WIN_R = 8
NA_WIN_C = 16
NA_BLOCK_ROWS = 2
NA_PAD_ROWS = NA_BLOCK_ROWS - 1
NA_HEADS_PER_STEP = 2
GQA_Q_HEADS = D_MODEL // HEAD_DIM
GQA_KV_HEADS = GQA_Q_HEADS // 4
GQA_GROUP = GQA_Q_HEADS // GQA_KV_HEADS
ROPE_THETA = 10000.0
N_MIXERS = 3

HALO = 16
LANES = 128
NA_TABLE_LANES = -(-(2 * NA_WIN_R - 1 + 2 * NA_PAD_ROWS) * GRID_W // LANES) * LANES
NORM_CHUNK = 64
LOG2E = 1.4426950408889634
MIB = 1024 * 1024
VMEM_LIMIT = 48 * MIB

BF16 = jnp.bfloat16
F32 = jnp.float32


def _params(*sem, vmem=VMEM_LIMIT):
    return pltpu.CompilerParams(dimension_semantics=sem, vmem_limit_bytes=vmem)


def _rms(x, gain):
    return x * lax.rsqrt(jnp.mean(x * x, axis=-1, keepdims=True) + EPS) * gain


def _norm_matmul_kernel(x_ref, g_ref, w_ref, o_ref, a_s):
    @pl.when(pl.program_id(1) == 0)
    def _():
        for c in range(0, a_s.shape[0], NORM_CHUNK):
            a_s[c:c + NORM_CHUNK, :] = _rms(x_ref[c:c + NORM_CHUNK, :], g_ref[...]).astype(BF16)

    o_ref[...] = jnp.dot(a_s[...], w_ref[...], preferred_element_type=F32).astype(o_ref.dtype)


def _norm_matmul(x, gain, w, *, tm=1024, tn=1024):
    t, d = x.shape
    n = w.shape[1]
    return pl.pallas_call(
        _norm_matmul_kernel,
        grid=(t // tm, n // tn),
        in_specs=[
            pl.BlockSpec((tm, d), lambda i, j: (i, 0)),
            pl.BlockSpec((1, d), lambda i, j: (0, 0)),
            pl.BlockSpec((d, tn), lambda i, j: (0, j)),
        ],
        out_specs=pl.BlockSpec((tm, tn), lambda i, j: (i, j)),
        out_shape=jax.ShapeDtypeStruct((t, n), BF16),
        scratch_shapes=[pltpu.VMEM((tm, d), BF16)],
        compiler_params=_params("parallel", "arbitrary"),
        name="norm_matmul",
    )(x, gain.reshape(1, d), w)


def _matmul_resid_kernel(a_ref, w_ref, r_ref, o_ref):
    o_ref[...] = r_ref[...] + jnp.dot(a_ref[...], w_ref[...], preferred_element_type=F32)


def _matmul_resid(a, w, resid, *, tm=512, tn=2048):
    t, k = a.shape
    n = w.shape[1]
    return pl.pallas_call(
        _matmul_resid_kernel,
        grid=(t // tm, n // tn),
        in_specs=[
            pl.BlockSpec((tm, k), lambda i, j: (i, 0)),
            pl.BlockSpec((k, tn), lambda i, j: (0, j)),
            pl.BlockSpec((tm, tn), lambda i, j: (i, j)),
        ],
        out_specs=pl.BlockSpec((tm, tn), lambda i, j: (i, j)),
        out_shape=jax.ShapeDtypeStruct((t, n), F32),
        compiler_params=_params("parallel", "arbitrary"),
        name="matmul_resid",
    )(a, w, resid)


def _ffn_kernel(h_ref, hp_ref, hn_ref, g_ref, wg_ref, wu_ref, cwg_ref, cwu_ref,
                cbg_ref, cbu_ref, wd_ref, fg_ref, *rest, tm, nj, seq_tiles, final_norm, cast_steps):
    n_cast = len(cast_steps)
    cast_in, o_ref, cast_out, a_s = rest[:n_cast], rest[n_cast], rest[n_cast + 1:-1], rest[-1]
    i = pl.program_id(0)
    j = pl.program_id(1)
    rows = tm + HALO

    for src, dst, steps in zip(cast_in, cast_out, cast_steps):
        if steps == nj:
            dst[...] = src[...].astype(BF16)
        else:
            @pl.when(j < steps)
            def _(src=src, dst=dst):
                dst[...] = src[...].astype(BF16)

    @pl.when(j == 0)
    def _():
        gain = g_ref[...]
        for c in range(0, tm, NORM_CHUNK):
            x = h_ref[c:c + NORM_CHUNK, :]
            o_ref[c:c + NORM_CHUNK, :] = x
            a_s[c:c + NORM_CHUNK, :] = _rms(x, gain).astype(BF16)
        first = (i % seq_tiles) == 0
        last = (i % seq_tiles) == seq_tiles - 1
        r = lax.broadcasted_iota(jnp.int32, hp_ref.shape, 0)
        halo = jnp.where((r == 0) & ~last, _rms(hn_ref[...], gain),
                         jnp.where((r == HALO - 1) & ~first, _rms(hp_ref[...], gain), 0.0))
        a_s[tm:rows, :] = halo.astype(BF16)

    a = a_s[...]

    def conv_branch(w_ref, cw_ref, cb_ref):
        up = jnp.dot(a, w_ref[...], preferred_element_type=F32)
        cw = cw_ref[...]
        c = (pltpu.roll(up, 1, 0) * cw[0:1] + up * cw[1:2]
             + pltpu.roll(up, rows - 1, 0) * cw[2:3])
        return c[0:tm] + cb_ref[...]

    g = conv_branch(wg_ref, cwg_ref, cbg_ref)
    u = conv_branch(wu_ref, cwu_ref, cbu_ref)
    half_g = 0.5 * g
    act = ((half_g + half_g * jnp.tanh(half_g)) * u).astype(BF16)
    o_ref[...] += jnp.dot(act, wd_ref[...], preferred_element_type=F32)

    if final_norm:
        @pl.when(j == pl.num_programs(1) - 1)
        def _():
            for c in range(0, tm, NORM_CHUNK):
                o_ref[c:c + NORM_CHUNK, :] = _rms(o_ref[c:c + NORM_CHUNK, :], fg_ref[...])


def _ffn(h, gain, w_up, conv_w, conv_b, w_down, layer, final_gain, side_casts,
         *, final_norm, tm=1024, tn=512):
    t, d = h.shape
    d_ff = w_down.shape[0]
    ni, nj = t // tm, d_ff // tn
    seq_tiles = SEQ // tm
    hb = tm // HALO
    n_hblocks = t // HALO
    in_specs = [
        pl.BlockSpec((tm, d), lambda i, j: (i, 0)),
        pl.BlockSpec((HALO, d), lambda i, j: (jnp.maximum(i * hb - 1, 0), 0)),
        pl.BlockSpec((HALO, d), lambda i, j: (jnp.minimum((i + 1) * hb, n_hblocks - 1), 0)),
        pl.BlockSpec((1, d), lambda i, j: (0, 0)),
        pl.BlockSpec((d, tn), lambda i, j: (0, j)),
        pl.BlockSpec((d, tn), lambda i, j: (0, j + nj)),
        pl.BlockSpec((None, 3, tn), lambda i, j: (layer, 0, j)),
        pl.BlockSpec((None, 3, tn), lambda i, j: (layer, 0, j + nj)),
        pl.BlockSpec((None, 1, tn), lambda i, j: (layer, 0, j)),
        pl.BlockSpec((None, 1, tn), lambda i, j: (layer, 0, j + nj)),
        pl.BlockSpec((tn, d), lambda i, j: (j, 0)),
        pl.BlockSpec((1, d), lambda i, j: (0, 0)),
    ]
    operands = [h, h, h, gain.reshape(1, d), w_up, w_up, conv_w, conv_w, conv_b, conv_b, w_down,
                final_gain.reshape(1, d)]
    out_specs = [pl.BlockSpec((tm, d), lambda i, j: (i, 0))]
    out_shape = [jax.ShapeDtypeStruct((t, d), F32)]
    cast_specs, cast_steps = [], []
    for w, w_layer in side_casts:
        _, r, c = w.shape
        steps = max(s for s in range(1, nj + 1) if (c // LANES) % s == 0)
        blk = (r // ni, c // steps)
        cast_specs.append(pl.BlockSpec(
            (None,) + blk, lambda i, j, w_layer=w_layer, steps=steps: (w_layer, i, jnp.minimum(j, steps - 1))))
        out_specs.append(pl.BlockSpec(blk, lambda i, j, steps=steps: (i, jnp.minimum(j, steps - 1))))
        out_shape.append(jax.ShapeDtypeStruct((r, c), BF16))
        operands.append(w)
        cast_steps.append(steps)
    return pl.pallas_call(
        partial(_ffn_kernel, tm=tm, nj=nj, seq_tiles=seq_tiles, final_norm=final_norm,
                cast_steps=tuple(cast_steps)),
        grid=(ni, nj),
        in_specs=in_specs + cast_specs,
        out_specs=out_specs,
        out_shape=out_shape,
        scratch_shapes=[pltpu.VMEM((tm + HALO, d), BF16)],
        compiler_params=_params("parallel", "arbitrary", vmem=60 * MIB),
        name="conv_glu_ffn",
    )(*operands)


def _na_bias_table(rpb):
    heads, _, n_dc = rpb.shape
    cols = jnp.arange(GRID_W)
    c_start = jnp.clip(cols - NA_WIN_C // 2, 0, GRID_W - NA_WIN_C)
    col_valid = (cols[None, :] >= c_start[:, None]) & (cols[None, :] < c_start[:, None] + NA_WIN_C)
    offs = cols[None, :] - cols[:, None] + NA_WIN_C - 1
    onehot = (offs[None] == jnp.arange(n_dc)[:, None, None]).astype(F32)
    tbl = jnp.einsum('hdj,jqk->hqdk', rpb * LOG2E, onehot, precision=lax.Precision.HIGHEST)
    tbl = jnp.where(col_valid[None, :, None, :], tbl, NEG_INF).reshape(heads, GRID_W, -1)
    left = NA_PAD_ROWS * GRID_W
    pad = lambda x, l: jnp.pad(x, ((0, 0), (0, 0), (l, NA_TABLE_LANES - l - x.shape[-1])),
                               constant_values=NEG_INF)
    return jnp.stack([pad(tbl, left), pad(tbl, left - GRID_W)], axis=1).astype(F32)


def _with_ones_column(v):
    return jnp.concatenate([v, jnp.ones_like(v)], axis=1)


def _na_kernel(q_ref, k_ref, v_ref, b_ref, o_ref, s_s, p_s, v_s, *, group):
    union_rows = NA_WIN_R + NA_BLOCK_ROWS - 1
    union = union_rows * GRID_W
    bq = NA_BLOCK_ROWS * GRID_W
    n_groups = GRID_ROWS // (NA_BLOCK_ROWS * group)
    lanes = [slice(hd * HEAD_DIM, (hd + 1) * HEAD_DIM) for hd in range(NA_HEADS_PER_STEP)]
    for hd, ls in enumerate(lanes):
        v_s[hd] = _with_ones_column(v_ref[:, ls])

    def r_start(r):
        return min(max(r - NA_WIN_R // 2, 0), GRID_ROWS - NA_WIN_R)

    def k_start(blk):
        return min(r_start(blk * NA_BLOCK_ROWS), GRID_ROWS - union_rows)

    def row_bias(hd, r, start):
        first = NA_WIN_R - 1 + NA_PAD_ROWS - (r - start)
        lane0 = (first // 2) * 2 * GRID_W
        b = b_ref[hd, first % 2, :, lane0:lane0 + union]
        lo, hi = (r_start(r) - start) * GRID_W, (r_start(r) - start + NA_WIN_R) * GRID_W
        tiles = []
        for t0 in range(0, union, LANES):
            t1 = min(t0 + LANES, union)
            tile = b[:, t0:t1]
            if t1 <= lo or t0 >= hi:
                tile = jnp.full_like(tile, NEG_INF)
            elif t0 < lo or t1 > hi:
                lane = t0 + lax.broadcasted_iota(jnp.int32, tile.shape, 1)
                tile = jnp.where((lane >= lo) & (lane < hi), tile, NEG_INF)
            tiles.append(tile)
        return jnp.concatenate(tiles, axis=1)

    def blocks(g):
        return range(g * group, (g + 1) * group)

    def scores(hd, g):
        for blk in blocks(g):
            q0, start = blk * bq, k_start(blk)
            q = (q_ref[q0:q0 + bq, lanes[hd]].astype(F32) * (HEAD_DIM ** -0.5 * LOG2E)).astype(BF16)
            s = lax.dot_general(q, k_ref[start * GRID_W:start * GRID_W + union, lanes[hd]],
                                (((1,), (1,)), ((), ())), preferred_element_type=F32)
            bias = jnp.concatenate([row_bias(hd, blk * NA_BLOCK_ROWS + i, start)
                                    for i in range(NA_BLOCK_ROWS)], axis=0)
            s_s[hd, q0:q0 + bq, :] = s + bias

    def softmax(hd, g):
        rs = slice(g * group * bq, (g + 1) * group * bq)
        s = s_s[hd, rs, :]
        p_s[hd, rs, :] = jnp.exp2(s - jnp.max(s, axis=-1, keepdims=True)).astype(BF16)

    def weighted_values(hd, g):
        for blk in blocks(g):
            q0, k0 = blk * bq, k_start(blk) * GRID_W
            o = jnp.dot(p_s[hd, q0:q0 + bq, :], v_s[hd, k0:k0 + union, :],
                        preferred_element_type=F32)
            o_ref[q0:q0 + bq, lanes[hd]] = (
                o[:, :HEAD_DIM] / o[:, HEAD_DIM:HEAD_DIM + 1]).astype(o_ref.dtype)

    stages = [(hd, g) for hd in range(NA_HEADS_PER_STEP) for g in range(n_groups)]
    scores(*stages[0])
    for n, stage in enumerate(stages):
        if n + 1 < len(stages):
            scores(*stages[n + 1])
        softmax(*stage)
        weighted_values(*stage)


def _na_attention(qkv, bias_tbl, batch, *, group=4):
    t = qkv.shape[0]
    union = (NA_WIN_R + NA_BLOCK_ROWS - 1) * GRID_W
    hps = NA_HEADS_PER_STEP
    n_steps = NA_HEADS // hps
    blk = lambda off: pl.BlockSpec((SEQ, hps * HEAD_DIM), lambda h, b: (b, h + off))
    return pl.pallas_call(
        partial(_na_kernel, group=group),
        grid=(n_steps, batch),
        in_specs=[
            blk(0), blk(n_steps), blk(2 * n_steps),
            pl.BlockSpec((hps, 2, GRID_W, NA_TABLE_LANES), lambda h, b: (h, 0, 0, 0)),
        ],
        out_specs=pl.BlockSpec((SEQ, hps * HEAD_DIM), lambda h, b: (b, h)),
        out_shape=jax.ShapeDtypeStruct((t, NA_HEADS * HEAD_DIM), BF16),
        scratch_shapes=[pltpu.VMEM((hps, SEQ, union), F32), pltpu.VMEM((hps, SEQ, union), BF16),
                        pltpu.VMEM((hps, SEQ, 2 * HEAD_DIM), BF16)],
        compiler_params=_params("parallel", "arbitrary"),
        name="neighborhood_attention",
    )(qkv, qkv, qkv, bias_tbl)


def _sc_out_kernel(gb_ref, gc_ref, hh_ref, gcp_ref, hhp_ref, gcn_ref, hhn_ref, cw_ref,
                   w_ref, r_ref, o_ref, *, tm, seq_tiles, chunk):
    i = pl.program_id(0)
    first = (i % seq_tiles) == 0
    last = (i % seq_tiles) == seq_tiles - 1
    rows = lax.broadcasted_iota(jnp.int32, (tm, chunk), 0)
    acc = r_ref[...]
    for c in range(0, w_ref.shape[0], chunk):
        cs = slice(c, c + chunk)
        p = gc_ref[:, cs].astype(F32) * hh_ref[:, cs].astype(F32)
        prev = gcp_ref[HALO - 1:HALO, cs].astype(F32) * hhp_ref[HALO - 1:HALO, cs].astype(F32)
        nxt = gcn_ref[0:1, cs].astype(F32) * hhn_ref[0:1, cs].astype(F32)
        prev = jnp.where(first, 0.0, prev)
        nxt = jnp.where(last, 0.0, nxt)
        pm1 = jnp.where(rows == 0, prev, pltpu.roll(p, 1, 0))
        pp1 = jnp.where(rows == tm - 1, nxt, pltpu.roll(p, tm - 1, 0))
        cw = cw_ref[:, cs]
        conv = pm1 * cw[0:1] + p * cw[1:2] + pp1 * cw[2:3]
        y = (gb_ref[:, cs].astype(F32) * conv).astype(BF16)
        acc = acc + jnp.dot(y, w_ref[cs, :], preferred_element_type=F32)
    o_ref[...] = acc


def _sc_out(z, conv_w, w_out, resid, *, tm=512, chunk=512):
    t = z.shape[0]
    d, n = w_out.shape
    seq_tiles = SEQ // tm
    hb = tm // HALO
    n_hblocks = t // HALO
    main = lambda c: pl.BlockSpec((tm, d), lambda i: (i, c))
    prev = lambda c: pl.BlockSpec((HALO, d), lambda i: (jnp.maximum(i * hb - 1, 0), c))
    nxt = lambda c: pl.BlockSpec((HALO, d), lambda i: (jnp.minimum((i + 1) * hb, n_hblocks - 1), c))
    return pl.pallas_call(
        partial(_sc_out_kernel, tm=tm, seq_tiles=seq_tiles, chunk=chunk),
        grid=(t // tm,),
        in_specs=[
            main(0), main(1), main(2), prev(1), prev(2), nxt(1), nxt(2),
            pl.BlockSpec((3, d), lambda i: (0, 0)),
            pl.BlockSpec((d, n), lambda i: (0, 0)),
            pl.BlockSpec((tm, n), lambda i: (i, 0)),
        ],
        out_specs=pl.BlockSpec((tm, n), lambda i: (i, 0)),
        out_shape=jax.ShapeDtypeStruct((t, n), F32),
        compiler_params=_params("parallel"),
        name="short_conv_out",
    )(z, z, z, z, z, z, z, conv_w, w_out, resid)


def _rope_tables():
    t = jnp.arange(SEQ)
    row = (t // GRID_W).astype(F32)[:, None]
    col = (t % GRID_W).astype(F32)[:, None]
    half = HEAD_DIM // 2
    inv = ROPE_THETA ** (-jnp.arange(0, half, 2, dtype=F32) / half)
    ang = jnp.concatenate([row * inv, row * inv, col * inv, col * inv], axis=-1)
    cos, sin = jnp.cos(ang), jnp.sin(ang)
    lane = jnp.arange(HEAD_DIM)[None, :]
    lower = (lane % half) < (half // 2)
    sin_lo = jnp.where(lower, -sin, 0.0)
    sin_hi = jnp.where(lower, 0.0, sin)
    return cos, sin_lo, sin_hi


def _norm_rope(x, gain, cos, sin_lo, sin_hi):
    y = _rms(x, gain)
    quarter = HEAD_DIM // 4
    return (y * cos + pltpu.roll(y, HEAD_DIM - quarter, 1) * sin_lo
            + pltpu.roll(y, quarter, 1) * sin_hi)


def _gqa_kernel(q_ref, k_ref, v_ref, qn_ref, kn_ref, cq_ref, slq_ref, shq_ref,
                ck_ref, slk_ref, shk_ref, o_ref, kt_s, v_s, *, row_parts):
    @pl.when(pl.program_id(2) == 0)
    def _():
        k = _norm_rope(k_ref[...].astype(F32), kn_ref[...], ck_ref[...], slk_ref[...], shk_ref[...])
        kt_s[...] = k.T.astype(BF16)
        v_s[...] = _with_ones_column(v_ref[...])

    cos, sin_lo, sin_hi = cq_ref[...], slq_ref[...], shq_ref[...]
    tq = q_ref.shape[0]
    units = [(slice(r, r + tq // row_parts), slice(g * HEAD_DIM, (g + 1) * HEAD_DIM))
             for g in range(GQA_GROUP) for r in range(0, tq, tq // row_parts)]

    def scores(rs, hs):
        q = _norm_rope(q_ref[rs, hs].astype(F32), qn_ref[...], cos[rs], sin_lo[rs], sin_hi[rs])
        q = (q * (HEAD_DIM ** -0.5 * LOG2E)).astype(BF16)
        return jnp.dot(q, kt_s[...], preferred_element_type=F32)

    def weighted_values(rs, hs, s):
        p = jnp.exp2(s - jnp.max(s, axis=-1, keepdims=True)).astype(BF16)
        o = jnp.dot(p, v_s[...], preferred_element_type=F32)
        o_ref[rs, hs] = (o[:, :HEAD_DIM] / o[:, HEAD_DIM:HEAD_DIM + 1]).astype(o_ref.dtype)

    s_next = scores(*units[0])
    for n, unit in enumerate(units):
        s = s_next
        if n + 1 < len(units):
            s_next = scores(*units[n + 1])
        weighted_values(*unit, s)


def _gqa_attention(qkv, q_norm, k_norm, batch, *, tq=1024, row_parts=4):
    t = qkv.shape[0]
    nq = SEQ // tq
    gw = GQA_GROUP * HEAD_DIM
    cos, sin_lo, sin_hi = _rope_tables()
    qtab = pl.BlockSpec((tq, HEAD_DIM), lambda b, h, i: (i, 0))
    ktab = pl.BlockSpec((SEQ, HEAD_DIM), lambda b, h, i: (0, 0))
    vec = pl.BlockSpec((1, HEAD_DIM), lambda b, h, i: (0, 0))
    return pl.pallas_call(
        partial(_gqa_kernel, row_parts=row_parts),
        grid=(batch, GQA_KV_HEADS, nq),
        in_specs=[
            pl.BlockSpec((tq, gw), lambda b, h, i: (b * nq + i, h)),
            pl.BlockSpec((SEQ, HEAD_DIM), lambda b, h, i: (b, GQA_Q_HEADS + h)),
            pl.BlockSpec((SEQ, HEAD_DIM), lambda b, h, i: (b, GQA_Q_HEADS + GQA_KV_HEADS + h)),
            vec, vec, qtab, qtab, qtab, ktab, ktab, ktab,
        ],
        out_specs=pl.BlockSpec((tq, gw), lambda b, h, i: (b * nq + i, h)),
        out_shape=jax.ShapeDtypeStruct((t, GQA_Q_HEADS * HEAD_DIM), BF16),
        scratch_shapes=[pltpu.VMEM((HEAD_DIM, SEQ), BF16), pltpu.VMEM((SEQ, 2 * HEAD_DIM), BF16)],
        compiler_params=_params("parallel", "parallel", "arbitrary"),
        name="gqa_attention",
    )(qkv, qkv, qkv, q_norm.reshape(1, -1), k_norm.reshape(1, -1),
      cos, sin_lo, sin_hi, cos, sin_lo, sin_hi)


def kernel(x, mix_norm, ffn_norm, final_norm, na_w_qkv, na_rpb, na_w_o, sc_w_in, sc_conv_w,
           sc_w_out, gqa_w_qkv, gqa_q_norm, gqa_k_norm, gqa_w_o, ffn_w_up, ffn_conv_w,
           ffn_conv_b, ffn_w_down):
    batch, seq, d = x.shape
    assert (seq, d) == (SEQ, D_MODEL)
    depth = mix_norm.shape[0]
    h = x.reshape(batch * seq, d)
    mixer_f32 = ((na_w_qkv, na_w_o), (sc_w_in, sc_w_out), (gqa_w_qkv, gqa_w_o))

    def layer_weights_f32(i):
        return [(w, i // N_MIXERS) for w in mixer_f32[i % N_MIXERS]] + [(ffn_w_up, i), (ffn_w_down, i)]

    w_in, w_out, w_up, w_down = [w[l].astype(BF16) for w, l in layer_weights_f32(0)]
    ffn_conv_b = ffn_conv_b.reshape(depth, 1, -1)
    for i in range(depth):
        m, j = i % N_MIXERS, i // N_MIXERS
        if m == 0:
            qkv = _norm_matmul(h, mix_norm[i], w_in)
            att = _na_attention(qkv, _na_bias_table(na_rpb[j]), batch)
            h = _matmul_resid(att, w_out, h)
        elif m == 1:
            z = _norm_matmul(h, mix_norm[i], w_in)
            h = _sc_out(z, sc_conv_w[j], w_out, h)
        else:
            qkv = _norm_matmul(h, mix_norm[i], w_in)
            att = _gqa_attention(qkv, gqa_q_norm[j], gqa_k_norm[j], batch)
            h = _matmul_resid(att, w_out, h)
        side_casts = layer_weights_f32(i + 1) if i + 1 < depth else []
        h, *next_weights = _ffn(h, ffn_norm[i], w_up, ffn_conv_w, ffn_conv_b, w_down, i, final_norm,
                                side_casts, final_norm=(i == depth - 1))
        if next_weights:
            w_in, w_out, w_up, w_down = next_weights
    return h.reshape(batch, seq, d)
```

```python
from functools import partial

import jax
import jax.numpy as jnp
from jax import lax
from jax.experimental import pallas as pl
from jax.experimental.pallas import tpu as pltpu

D_MODEL = 2048
SEQ = 2048
GRID_W = 64
GRID_ROWS = SEQ // GRID_W
HEAD_DIM = 128
EPS = 1e-6
NEG_INF = -1e30
NA_HEADS = D_MODEL // HEAD_DIM
NA_WIN_R = 8
NA_WIN_C = 16
NA_BLOCK_ROWS = 2
NA_PAD_ROWS = NA_BLOCK_ROWS - 1
NA_HEADS_PER_STEP = 2
GQA_Q_HEADS = D_MODEL // HEAD_DIM
GQA_KV_HEADS = GQA_Q_HEADS // 4
GQA_GROUP = GQA_Q_HEADS // GQA_KV_HEADS
ROPE_THETA = 10000.0
N_MIXERS = 3

HALO = 16
LANES = 128
NA_TABLE_LANES = -(-(2 * NA_WIN_R - 1 + 2 * NA_PAD_ROWS) * GRID_W // LANES) * LANES
NORM_CHUNK = 64
LOG2E = 1.4426950408889634
MIB = 1024 * 1024
VMEM_LIMIT = 48 * MIB

BF16 = jnp.bfloat16
F32 = jnp.float32


def _params(*sem, vmem=VMEM_LIMIT):
    return pltpu.CompilerParams(dimension_semantics=sem, vmem_limit_bytes=vmem)


def _rms(x, gain):
    return x * lax.rsqrt(jnp.mean(x * x, axis=-1, keepdims=True) + EPS) * gain


def _side_cast_specs(side_casts, ni, nj):
    in_specs, out_specs, out_shapes, operands, steps_per = [], [], [], [], []
    for w, w_layer in side_casts:
        _, r, c = w.shape
        steps = max(s for s in range(1, nj + 1) if (c // LANES) % s == 0)
        blk = (r // ni, c // steps)
        in_specs.append(pl.BlockSpec(
            (None,) + blk, lambda i, j, w_layer=w_layer, steps=steps: (w_layer, i, jnp.minimum(j, steps - 1))))
        out_specs.append(pl.BlockSpec(blk, lambda i, j, steps=steps: (i, jnp.minimum(j, steps - 1))))
        out_shapes.append(jax.ShapeDtypeStruct((r, c), BF16))
        operands.append(w)
        steps_per.append(steps)
    return in_specs, out_specs, out_shapes, operands, tuple(steps_per)


def _run_side_casts(cast_in, cast_out, cast_steps, j, nj):
    for src, dst, steps in zip(cast_in, cast_out, cast_steps):
        if steps == nj:
            dst[...] = src[...].astype(BF16)
        else:
            @pl.when(j < steps)
            def _(src=src, dst=dst):
                dst[...] = src[...].astype(BF16)


def _norm_matmul_kernel(x_ref, g_ref, w_ref, *rest, nj, cast_steps):
    n_cast = len(cast_steps)
    cast_in, o_ref, cast_out, a_s = rest[:n_cast], rest[n_cast], rest[n_cast + 1:-1], rest[-1]
    j = pl.program_id(1)
    _run_side_casts(cast_in, cast_out, cast_steps, j, nj)

    @pl.when(j == 0)
    def _():
        for c in range(0, a_s.shape[0], NORM_CHUNK):
            a_s[c:c + NORM_CHUNK, :] = _rms(x_ref[c:c + NORM_CHUNK, :], g_ref[...]).astype(BF16)

    o_ref[...] = jnp.dot(a_s[...], w_ref[...], preferred_element_type=F32).astype(o_ref.dtype)


def _norm_matmul(x, gain, w, side_casts=(), *, tm=1024, tn=1024):
    t, d = x.shape
    n = w.shape[1]
    ni, nj = t // tm, n // tn
    cast_in, cast_out, cast_shapes, cast_operands, cast_steps = _side_cast_specs(side_casts, ni, nj)
    return pl.pallas_call(
        partial(_norm_matmul_kernel, nj=nj, cast_steps=cast_steps),
        grid=(ni, nj),
        in_specs=[
            pl.BlockSpec((tm, d), lambda i, j: (i, 0)),
            pl.BlockSpec((1, d), lambda i, j: (0, 0)),
            pl.BlockSpec((d, tn), lambda i, j: (0, j)),
        ] + cast_in,
        out_specs=[pl.BlockSpec((tm, tn), lambda i, j: (i, j))] + cast_out,
        out_shape=[jax.ShapeDtypeStruct((t, n), BF16)] + cast_shapes,
        scratch_shapes=[pltpu.VMEM((tm, d), BF16)],
        compiler_params=_params("parallel", "arbitrary"),
        name="norm_matmul",
    )(x, gain.reshape(1, d), w, *cast_operands)


def _matmul_resid_kernel(a_ref, w_ref, r_ref, o_ref):
    o_ref[...] = r_ref[...] + jnp.dot(a_ref[...], w_ref[...], preferred_element_type=F32)


def _matmul_resid(a, w, resid, *, tm=512, tn=2048):
    t, k = a.shape
    n = w.shape[1]
    return pl.pallas_call(
        _matmul_resid_kernel,
        grid=(t // tm, n // tn),
        in_specs=[
            pl.BlockSpec((tm, k), lambda i, j: (i, 0)),
            pl.BlockSpec((k, tn), lambda i, j: (0, j)),
            pl.BlockSpec((tm, tn), lambda i, j: (i, j)),
        ],
        out_specs=pl.BlockSpec((tm, tn), lambda i, j: (i, j)),
        out_shape=jax.ShapeDtypeStruct((t, n), F32),
        compiler_params=_params("parallel", "arbitrary"),
        name="matmul_resid",
    )(a, w, resid)


def _ffn_kernel(h_ref, hp_ref, hn_ref, g_ref, wg_ref, wu_ref, cwg_ref, cwu_ref,
                cbg_ref, cbu_ref, wd_ref, fg_ref, *rest, tm, nj, seq_tiles, final_norm, cast_steps):
    n_cast = len(cast_steps)
    cast_in, o_ref, cast_out, a_s = rest[:n_cast], rest[n_cast], rest[n_cast + 1:-1], rest[-1]
    i = pl.program_id(0)
    j = pl.program_id(1)
    rows = tm + HALO

    _run_side_casts(cast_in, cast_out, cast_steps, j, nj)

    @pl.when(j == 0)
    def _():
        gain = g_ref[...]
        for c in range(0, tm, NORM_CHUNK):
            x = h_ref[c:c + NORM_CHUNK, :]
            o_ref[c:c + NORM_CHUNK, :] = x
            a_s[c:c + NORM_CHUNK, :] = _rms(x, gain).astype(BF16)
        first = (i % seq_tiles) == 0
        last = (i % seq_tiles) == seq_tiles - 1
        r = lax.broadcasted_iota(jnp.int32, hp_ref.shape, 0)
        halo = jnp.where((r == 0) & ~last, _rms(hn_ref[...], gain),
                         jnp.where((r == HALO - 1) & ~first, _rms(hp_ref[...], gain), 0.0))
        a_s[tm:rows, :] = halo.astype(BF16)

    a = a_s[...]

    def conv_branch(w_ref, cw_ref, cb_ref):
        up = jnp.dot(a, w_ref[...], preferred_element_type=F32)
        cw = cw_ref[...]
        c = (pltpu.roll(up, 1, 0) * cw[0:1] + up * cw[1:2]
             + pltpu.roll(up, rows - 1, 0) * cw[2:3])
        return c[0:tm] + cb_ref[...]

    g = conv_branch(wg_ref, cwg_ref, cbg_ref)
    u = conv_branch(wu_ref, cwu_ref, cbu_ref)
    half_g = 0.5 * g
    act = ((half_g + half_g * jnp.tanh(half_g)) * u).astype(BF16)
    o_ref[...] += jnp.dot(act, wd_ref[...], preferred_element_type=F32)

    if final_norm:
        @pl.when(j == pl.num_programs(1) - 1)
        def _():
            for c in range(0, tm, NORM_CHUNK):
                o_ref[c:c + NORM_CHUNK, :] = _rms(o_ref[c:c + NORM_CHUNK, :], fg_ref[...])


def _ffn(h, gain, w_up, conv_w, conv_b, w_down, layer, final_gain, side_casts,
         *, final_norm, tm=1024, tn=512):
    t, d = h.shape
    d_ff = w_down.shape[0]
    ni, nj = t // tm, d_ff // tn
    seq_tiles = SEQ // tm
    hb = tm // HALO
    n_hblocks = t // HALO
    in_specs = [
        pl.BlockSpec((tm, d), lambda i, j: (i, 0)),
        pl.BlockSpec((HALO, d), lambda i, j: (jnp.maximum(i * hb - 1, 0), 0)),
        pl.BlockSpec((HALO, d), lambda i, j: (jnp.minimum((i + 1) * hb, n_hblocks - 1), 0)),
        pl.BlockSpec((1, d), lambda i, j: (0, 0)),
        pl.BlockSpec((d, tn), lambda i, j: (0, j)),
        pl.BlockSpec((d, tn), lambda i, j: (0, j + nj)),
        pl.BlockSpec((None, 3, tn), lambda i, j: (layer, 0, j)),
        pl.BlockSpec((None, 3, tn), lambda i, j: (layer, 0, j + nj)),
        pl.BlockSpec((None, 1, tn), lambda i, j: (layer, 0, j)),
        pl.BlockSpec((None, 1, tn), lambda i, j: (layer, 0, j + nj)),
        pl.BlockSpec((tn, d), lambda i, j: (j, 0)),
        pl.BlockSpec((1, d), lambda i, j: (0, 0)),
    ]
    operands = [h, h, h, gain.reshape(1, d), w_up, w_up, conv_w, conv_w, conv_b, conv_b, w_down,
                final_gain.reshape(1, d)]
    out_specs = [pl.BlockSpec((tm, d), lambda i, j: (i, 0))]
    out_shape = [jax.ShapeDtypeStruct((t, d), F32)]
    cast_in, cast_out, cast_shapes, cast_operands, cast_steps = _side_cast_specs(side_casts, ni, nj)
    return pl.pallas_call(
        partial(_ffn_kernel, tm=tm, nj=nj, seq_tiles=seq_tiles, final_norm=final_norm,
                cast_steps=cast_steps),
        grid=(ni, nj),
        in_specs=in_specs + cast_in,
        out_specs=out_specs + cast_out,
        out_shape=out_shape + cast_shapes,
        scratch_shapes=[pltpu.VMEM((tm + HALO, d), BF16)],
        compiler_params=_params("parallel", "arbitrary", vmem=60 * MIB),
        name="conv_glu_ffn",
    )(*operands, *cast_operands)


def _na_bias_table(rpb):
    heads, _, n_dc = rpb.shape
    cols = jnp.arange(GRID_W)
    c_start = jnp.clip(cols - NA_WIN_C // 2, 0, GRID_W - NA_WIN_C)
    col_valid = (cols[None, :] >= c_start[:, None]) & (cols[None, :] < c_start[:, None] + NA_WIN_C)
    offs = cols[None, :] - cols[:, None] + NA_WIN_C - 1
    onehot = (offs[None] == jnp.arange(n_dc)[:, None, None]).astype(F32)
    tbl = jnp.einsum('hdj,jqk->hqdk', rpb * LOG2E, onehot, precision=lax.Precision.HIGHEST)
    tbl = jnp.where(col_valid[None, :, None, :], tbl, NEG_INF).reshape(heads, GRID_W, -1)
    left = NA_PAD_ROWS * GRID_W
    pad = lambda x, l: jnp.pad(x, ((0, 0), (0, 0), (l, NA_TABLE_LANES - l - x.shape[-1])),
                               constant_values=NEG_INF)
    return jnp.stack([pad(tbl, left), pad(tbl, left - GRID_W)], axis=1).astype(F32)


def _with_ones_column(v):
    return jnp.concatenate([v, jnp.ones_like(v)], axis=1)


def _na_kernel(q_ref, k_ref, v_ref, b_ref, o_ref, s_s, p_s, v_s, *, group):
    union_rows = NA_WIN_R + NA_BLOCK_ROWS - 1
    union = union_rows * GRID_W
    bq = NA_BLOCK_ROWS * GRID_W
    n_groups = GRID_ROWS // (NA_BLOCK_ROWS * group)
    lanes = [slice(hd * HEAD_DIM, (hd + 1) * HEAD_DIM) for hd in range(NA_HEADS_PER_STEP)]
    for hd, ls in enumerate(lanes):
        v_s[hd] = _with_ones_column(v_ref[:, ls])

    def r_start(r):
        return min(max(r - NA_WIN_R // 2, 0), GRID_ROWS - NA_WIN_R)

    def k_start(blk):
        return min(r_start(blk * NA_BLOCK_ROWS), GRID_ROWS - union_rows)

    def row_bias(hd, r, start):
        first = NA_WIN_R - 1 + NA_PAD_ROWS - (r - start)
        lane0 = (first // 2) * 2 * GRID_W
        b = b_ref[hd, first % 2, :, lane0:lane0 + union]
        lo, hi = (r_start(r) - start) * GRID_W, (r_start(r) - start + NA_WIN_R) * GRID_W
        tiles = []
        for t0 in range(0, union, LANES):
            t1 = min(t0 + LANES, union)
            tile = b[:, t0:t1]
            if t1 <= lo or t0 >= hi:
                tile = jnp.full_like(tile, NEG_INF)
            elif t0 < lo or t1 > hi:
                lane = t0 + lax.broadcasted_iota(jnp.int32, tile.shape, 1)
                tile = jnp.where((lane >= lo) & (lane < hi), tile, NEG_INF)
            tiles.append(tile)
        return jnp.concatenate(tiles, axis=1)

    def blocks(g):
        return range(g * group, (g + 1) * group)

    def scores(hd, g):
        for blk in blocks(g):
            q0, start = blk * bq, k_start(blk)
            q = (q_ref[q0:q0 + bq, lanes[hd]].astype(F32) * (HEAD_DIM ** -0.5 * LOG2E)).astype(BF16)
            s = lax.dot_general(q, k_ref[start * GRID_W:start * GRID_W + union, lanes[hd]],
                                (((1,), (1,)), ((), ())), preferred_element_type=F32)
            bias = jnp.concatenate([row_bias(hd, blk * NA_BLOCK_ROWS + i, start)
                                    for i in range(NA_BLOCK_ROWS)], axis=0)
            s_s[hd, q0:q0 + bq, :] = s + bias

    def softmax(hd, g):
        rs = slice(g * group * bq, (g + 1) * group * bq)
        s = s_s[hd, rs, :]
        p_s[hd, rs, :] = jnp.exp2(s - jnp.max(s, axis=-1, keepdims=True)).astype(BF16)

    def weighted_values(hd, g):
        for blk in blocks(g):
            q0, k0 = blk * bq, k_start(blk) * GRID_W
            o = jnp.dot(p_s[hd, q0:q0 + bq, :], v_s[hd, k0:k0 + union, :],
                        preferred_element_type=F32)
            o_ref[q0:q0 + bq, lanes[hd]] = (
                o[:, :HEAD_DIM] / o[:, HEAD_DIM:HEAD_DIM + 1]).astype(o_ref.dtype)

    stages = [(hd, g) for hd in range(NA_HEADS_PER_STEP) for g in range(n_groups)]
    scores(*stages[0])
    for n, stage in enumerate(stages):
        if n + 1 < len(stages):
            scores(*stages[n + 1])
        softmax(*stage)
        weighted_values(*stage)


def _na_attention(qkv, bias_tbl, batch, *, group=4):
    t = qkv.shape[0]
    union = (NA_WIN_R + NA_BLOCK_ROWS - 1) * GRID_W
    hps = NA_HEADS_PER_STEP
    n_steps = NA_HEADS // hps
    blk = lambda off: pl.BlockSpec((SEQ, hps * HEAD_DIM), lambda h, b: (b, h + off))
    return pl.pallas_call(
        partial(_na_kernel, group=group),
        grid=(n_steps, batch),
        in_specs=[
            blk(0), blk(n_steps), blk(2 * n_steps),
            pl.BlockSpec((hps, 2, GRID_W, NA_TABLE_LANES), lambda h, b: (h, 0, 0, 0)),
        ],
        out_specs=pl.BlockSpec((SEQ, hps * HEAD_DIM), lambda h, b: (b, h)),
        out_shape=jax.ShapeDtypeStruct((t, NA_HEADS * HEAD_DIM), BF16),
        scratch_shapes=[pltpu.VMEM((hps, SEQ, union), F32), pltpu.VMEM((hps, SEQ, union), BF16),
                        pltpu.VMEM((hps, SEQ, 2 * HEAD_DIM), BF16)],
        compiler_params=_params("parallel", "arbitrary"),
        name="neighborhood_attention",
    )(qkv, qkv, qkv, bias_tbl)


def _sc_out_kernel(gb_ref, gc_ref, hh_ref, gcp_ref, hhp_ref, gcn_ref, hhn_ref, cw_ref,
                   w_ref, r_ref, o_ref, *, tm, seq_tiles, chunk):
    i = pl.program_id(0)
    first = (i % seq_tiles) == 0
    last = (i % seq_tiles) == seq_tiles - 1
    rows = lax.broadcasted_iota(jnp.int32, (tm, chunk), 0)
    acc = r_ref[...]
    for c in range(0, w_ref.shape[0], chunk):
        cs = slice(c, c + chunk)
        p = gc_ref[:, cs].astype(F32) * hh_ref[:, cs].astype(F32)
        prev = gcp_ref[HALO - 1:HALO, cs].astype(F32) * hhp_ref[HALO - 1:HALO, cs].astype(F32)
        nxt = gcn_ref[0:1, cs].astype(F32) * hhn_ref[0:1, cs].astype(F32)
        prev = jnp.where(first, 0.0, prev)
        nxt = jnp.where(last, 0.0, nxt)
        pm1 = jnp.where(rows == 0, prev, pltpu.roll(p, 1, 0))
        pp1 = jnp.where(rows == tm - 1, nxt, pltpu.roll(p, tm - 1, 0))
        cw = cw_ref[:, cs]
        conv = pm1 * cw[0:1] + p * cw[1:2] + pp1 * cw[2:3]
        y = (gb_ref[:, cs].astype(F32) * conv).astype(BF16)
        acc = acc + jnp.dot(y, w_ref[cs, :], preferred_element_type=F32)
    o_ref[...] = acc


def _sc_out(z, conv_w, w_out, resid, *, tm=512, chunk=512):
    t = z.shape[0]
    d, n = w_out.shape
    seq_tiles = SEQ // tm
    hb = tm // HALO
    n_hblocks = t // HALO
    main = lambda c: pl.BlockSpec((tm, d), lambda i: (i, c))
    prev = lambda c: pl.BlockSpec((HALO, d), lambda i: (jnp.maximum(i * hb - 1, 0), c))
    nxt = lambda c: pl.BlockSpec((HALO, d), lambda i: (jnp.minimum((i + 1) * hb, n_hblocks - 1), c))
    return pl.pallas_call(
        partial(_sc_out_kernel, tm=tm, seq_tiles=seq_tiles, chunk=chunk),
        grid=(t // tm,),
        in_specs=[
            main(0), main(1), main(2), prev(1), prev(2), nxt(1), nxt(2),
            pl.BlockSpec((3, d), lambda i: (0, 0)),
            pl.BlockSpec((d, n), lambda i: (0, 0)),
            pl.BlockSpec((tm, n), lambda i: (i, 0)),
        ],
        out_specs=pl.BlockSpec((tm, n), lambda i: (i, 0)),
        out_shape=jax.ShapeDtypeStruct((t, n), F32),
        compiler_params=_params("parallel"),
        name="short_conv_out",
    )(z, z, z, z, z, z, z, conv_w, w_out, resid)


def _rope_tables():
    t = jnp.arange(SEQ)
    row = (t // GRID_W).astype(F32)[:, None]
    col = (t % GRID_W).astype(F32)[:, None]
    half = HEAD_DIM // 2
    inv = ROPE_THETA ** (-jnp.arange(0, half, 2, dtype=F32) / half)
    ang = jnp.concatenate([row * inv, row * inv, col * inv, col * inv], axis=-1)
    cos, sin = jnp.cos(ang), jnp.sin(ang)
    lane = jnp.arange(HEAD_DIM)[None, :]
    lower = (lane % half) < (half // 2)
    sin_lo = jnp.where(lower, -sin, 0.0)
    sin_hi = jnp.where(lower, 0.0, sin)
    return cos, sin_lo, sin_hi


def _norm_rope(x, gain, cos, sin_lo, sin_hi):
    y = _rms(x, gain)
    quarter = HEAD_DIM // 4
    return (y * cos + pltpu.roll(y, HEAD_DIM - quarter, 1) * sin_lo
            + pltpu.roll(y, quarter, 1) * sin_hi)


def _gqa_kernel(q_ref, k_ref, v_ref, qn_ref, kn_ref, cq_ref, slq_ref, shq_ref,
                ck_ref, slk_ref, shk_ref, o_ref, kt_s, v_s, *, row_parts):
    @pl.when(pl.program_id(2) == 0)
    def _():
        k = _norm_rope(k_ref[...].astype(F32), kn_ref[...], ck_ref[...], slk_ref[...], shk_ref[...])
        kt_s[...] = k.T.astype(BF16)
        v_s[...] = _with_ones_column(v_ref[...])

    cos, sin_lo, sin_hi = cq_ref[...], slq_ref[...], shq_ref[...]
    tq = q_ref.shape[0]
    units = [(slice(r, r + tq // row_parts), slice(g * HEAD_DIM, (g + 1) * HEAD_DIM))
             for g in range(GQA_GROUP) for r in range(0, tq, tq // row_parts)]

    def scores(rs, hs):
        q = _norm_rope(q_ref[rs, hs].astype(F32), qn_ref[...], cos[rs], sin_lo[rs], sin_hi[rs])
        q = (q * (HEAD_DIM ** -0.5 * LOG2E)).astype(BF16)
        return jnp.dot(q, kt_s[...], preferred_element_type=F32)

    def weighted_values(rs, hs, s):
        p = jnp.exp2(s - jnp.max(s, axis=-1, keepdims=True)).astype(BF16)
        o = jnp.dot(p, v_s[...], preferred_element_type=F32)
        o_ref[rs, hs] = (o[:, :HEAD_DIM] / o[:, HEAD_DIM:HEAD_DIM + 1]).astype(o_ref.dtype)

    s_next = scores(*units[0])
    for n, unit in enumerate(units):
        s = s_next
        if n + 1 < len(units):
            s_next = scores(*units[n + 1])
        weighted_values(*unit, s)


def _gqa_attention(qkv, q_norm, k_norm, batch, *, tq=1024, row_parts=4):
    t = qkv.shape[0]
    nq = SEQ // tq
    gw = GQA_GROUP * HEAD_DIM
    cos, sin_lo, sin_hi = _rope_tables()
    qtab = pl.BlockSpec((tq, HEAD_DIM), lambda b, h, i: (i, 0))
    ktab = pl.BlockSpec((SEQ, HEAD_DIM), lambda b, h, i: (0, 0))
    vec = pl.BlockSpec((1, HEAD_DIM), lambda b, h, i: (0, 0))
    return pl.pallas_call(
        partial(_gqa_kernel, row_parts=row_parts),
        grid=(batch, GQA_KV_HEADS, nq),
        in_specs=[
            pl.BlockSpec((tq, gw), lambda b, h, i: (b * nq + i, h)),
            pl.BlockSpec((SEQ, HEAD_DIM), lambda b, h, i: (b, GQA_Q_HEADS + h)),
            pl.BlockSpec((SEQ, HEAD_DIM), lambda b, h, i: (b, GQA_Q_HEADS + GQA_KV_HEADS + h)),
            vec, vec, qtab, qtab, qtab, ktab, ktab, ktab,
        ],
        out_specs=pl.BlockSpec((tq, gw), lambda b, h, i: (b * nq + i, h)),
        out_shape=jax.ShapeDtypeStruct((t, GQA_Q_HEADS * HEAD_DIM), BF16),
        scratch_shapes=[pltpu.VMEM((HEAD_DIM, SEQ), BF16), pltpu.VMEM((SEQ, 2 * HEAD_DIM), BF16)],
        compiler_params=_params("parallel", "parallel", "arbitrary"),
        name="gqa_attention",
    )(qkv, qkv, qkv, q_norm.reshape(1, -1), k_norm.reshape(1, -1),
      cos, sin_lo, sin_hi, cos, sin_lo, sin_hi)


def kernel(x, mix_norm, ffn_norm, final_norm, na_w_qkv, na_rpb, na_w_o, sc_w_in, sc_conv_w,
           sc_w_out, gqa_w_qkv, gqa_q_norm, gqa_k_norm, gqa_w_o, ffn_w_up, ffn_conv_w,
           ffn_conv_b, ffn_w_down):
    batch, seq, d = x.shape
    assert (seq, d) == (SEQ, D_MODEL)
    depth = mix_norm.shape[0]
    h = x.reshape(batch * seq, d)
    mixer_f32 = ((na_w_qkv, na_w_o), (sc_w_in, sc_w_out), (gqa_w_qkv, gqa_w_o))

    def layer_weights_f32(i):
        return [(w, i // N_MIXERS) for w in mixer_f32[i % N_MIXERS]] + [(ffn_w_up, i), (ffn_w_down, i)]

    w_in, w_out = [w[l].astype(BF16) for w, l in layer_weights_f32(0)[:2]]
    w_up = w_down = None
    ffn_conv_b = ffn_conv_b.reshape(depth, 1, -1)
    for i in range(depth):
        m, j = i % N_MIXERS, i // N_MIXERS
        first_casts = layer_weights_f32(0)[2:] if i == 0 else ()
        if m == 0:
            qkv, *cast = _norm_matmul(h, mix_norm[i], w_in, first_casts)
            att = _na_attention(qkv, _na_bias_table(na_rpb[j]), batch)
            h = _matmul_resid(att, w_out, h)
        elif m == 1:
            z, *cast = _norm_matmul(h, mix_norm[i], w_in, first_casts)
            h = _sc_out(z, sc_conv_w[j], w_out, h)
        else:
            qkv, *cast = _norm_matmul(h, mix_norm[i], w_in, first_casts)
            att = _gqa_attention(qkv, gqa_q_norm[j], gqa_k_norm[j], batch)
            h = _matmul_resid(att, w_out, h)
        if cast:
            w_up, w_down = cast
        side_casts = layer_weights_f32(i + 1) if i + 1 < depth else []
        h, *next_weights = _ffn(h, ffn_norm[i], w_up, ffn_conv_w, ffn_conv_b, w_down, i, final_norm,
                                side_casts, final_norm=(i == depth - 1))
        if next_weights:
            w_in, w_out, w_up, w_down = next_weights
    return h.reshape(batch, seq, d)
```

```python
from functools import partial

import jax
import jax.numpy as jnp
from jax import lax
from jax.experimental import pallas as pl
from jax.experimental.pallas import tpu as pltpu

D_MODEL = 2048
SEQ = 2048
GRID_W = 64
GRID_ROWS = SEQ // GRID_W
HEAD_DIM = 128
EPS = 1e-6
NEG_INF = -1e30
NA_HEADS = D_MODEL // HEAD_DIM
NA_WIN_R = 8
NA_WIN_C = 16
NA_BLOCK_ROWS = 2
NA_PAD_ROWS = NA_BLOCK_ROWS - 1
NA_HEADS_PER_STEP = 2
GQA_Q_HEADS = D_MODEL // HEAD_DIM
GQA_KV_HEADS = GQA_Q_HEADS // 4
GQA_GROUP = GQA_Q_HEADS // GQA_KV_HEADS
ROPE_THETA = 10000.0
N_MIXERS = 3

HALO = 16
LANES = 128
NA_TABLE_LANES = -(-(2 * NA_WIN_R - 1 + 2 * NA_PAD_ROWS) * GRID_W // LANES) * LANES
NORM_CHUNK = 64
LOG2E = 1.4426950408889634
MIB = 1024 * 1024
VMEM_LIMIT = 48 * MIB

BF16 = jnp.bfloat16
F32 = jnp.float32


def _params(*sem, vmem=VMEM_LIMIT):
    return pltpu.CompilerParams(dimension_semantics=sem, vmem_limit_bytes=vmem)


def _rms(x, gain):
    return x * lax.rsqrt(jnp.mean(x * x, axis=-1, keepdims=True) + EPS) * gain


def _side_cast_specs(side_casts, ni, nj):
    in_specs, out_specs, out_shapes, operands, steps_per = [], [], [], [], []
    for w, w_layer in side_casts:
        _, r, c = w.shape
        steps = max(s for s in range(1, nj + 1) if (c // LANES) % s == 0)
        blk = (r // ni, c // steps)
        in_specs.append(pl.BlockSpec(
            (None,) + blk, lambda i, j, w_layer=w_layer, steps=steps: (w_layer, i, jnp.minimum(j, steps - 1))))
        out_specs.append(pl.BlockSpec(blk, lambda i, j, steps=steps: (i, jnp.minimum(j, steps - 1))))
        out_shapes.append(jax.ShapeDtypeStruct((r, c), BF16))
        operands.append(w)
        steps_per.append(steps)
    return in_specs, out_specs, out_shapes, operands, tuple(steps_per)


def _run_side_casts(cast_in, cast_out, cast_steps, j, nj):
    for src, dst, steps in zip(cast_in, cast_out, cast_steps):
        if steps == nj:
            dst[...] = src[...].astype(BF16)
        else:
            @pl.when(j < steps)
            def _(src=src, dst=dst):
                dst[...] = src[...].astype(BF16)


def _norm_matmul_kernel(x_ref, g_ref, w_ref, *rest, nj, cast_steps):
    n_cast = len(cast_steps)
    cast_in, o_ref, cast_out, a_s = rest[:n_cast], rest[n_cast], rest[n_cast + 1:-1], rest[-1]
    j = pl.program_id(1)
    _run_side_casts(cast_in, cast_out, cast_steps, j, nj)

    @pl.when(j == 0)
    def _():
        for c in range(0, a_s.shape[0], NORM_CHUNK):
            a_s[c:c + NORM_CHUNK, :] = _rms(x_ref[c:c + NORM_CHUNK, :], g_ref[...]).astype(BF16)

    o_ref[...] = jnp.dot(a_s[...], w_ref[...], preferred_element_type=F32).astype(o_ref.dtype)


def _norm_matmul(x, gain, w, side_casts=(), *, tm=1024, tn=1024):
    t, d = x.shape
    n = w.shape[1]
    ni, nj = t // tm, n // tn
    cast_in, cast_out, cast_shapes, cast_operands, cast_steps = _side_cast_specs(side_casts, ni, nj)
    return pl.pallas_call(
        partial(_norm_matmul_kernel, nj=nj, cast_steps=cast_steps),
        grid=(ni, nj),
        in_specs=[
            pl.BlockSpec((tm, d), lambda i, j: (i, 0)),
            pl.BlockSpec((1, d), lambda i, j: (0, 0)),
            pl.BlockSpec((d, tn), lambda i, j: (0, j)),
        ] + cast_in,
        out_specs=[pl.BlockSpec((tm, tn), lambda i, j: (i, j))] + cast_out,
        out_shape=[jax.ShapeDtypeStruct((t, n), BF16)] + cast_shapes,
        scratch_shapes=[pltpu.VMEM((tm, d), BF16)],
        compiler_params=_params("parallel", "arbitrary"),
        name="norm_matmul",
    )(x, gain.reshape(1, d), w, *cast_operands)


def _matmul_resid_kernel(a_ref, w_ref, r_ref, o_ref):
    o_ref[...] = r_ref[...] + jnp.dot(a_ref[...], w_ref[...], preferred_element_type=F32)


def _matmul_resid(a, w, resid, *, tm=512, tn=2048):
    t, k = a.shape
    n = w.shape[1]
    return pl.pallas_call(
        _matmul_resid_kernel,
        grid=(t // tm, n // tn),
        in_specs=[
            pl.BlockSpec((tm, k), lambda i, j: (i, 0)),
            pl.BlockSpec((k, tn), lambda i, j: (0, j)),
            pl.BlockSpec((tm, tn), lambda i, j: (i, j)),
        ],
        out_specs=pl.BlockSpec((tm, tn), lambda i, j: (i, j)),
        out_shape=jax.ShapeDtypeStruct((t, n), F32),
        compiler_params=_params("parallel", "arbitrary"),
        name="matmul_resid",
    )(a, w, resid)


def _ffn_kernel(h_ref, hp_ref, hn_ref, g_ref, wg_ref, wu_ref, cwg_ref, cwu_ref,
                cbg_ref, cbu_ref, wd_ref, fg_ref, *rest, tm, nj, seq_tiles, final_norm, cast_steps):
    n_cast = len(cast_steps)
    cast_in, o_ref, cast_out, a_s = rest[:n_cast], rest[n_cast], rest[n_cast + 1:-1], rest[-1]
    i = pl.program_id(0)
    j = pl.program_id(1)
    rows = tm + HALO

    _run_side_casts(cast_in, cast_out, cast_steps, j, nj)

    @pl.when(j == 0)
    def _():
        gain = g_ref[...]
        for c in range(0, tm, NORM_CHUNK):
            x = h_ref[c:c + NORM_CHUNK, :]
            o_ref[c:c + NORM_CHUNK, :] = x
            a_s[c:c + NORM_CHUNK, :] = _rms(x, gain).astype(BF16)
        first = (i % seq_tiles) == 0
        last = (i % seq_tiles) == seq_tiles - 1
        r = lax.broadcasted_iota(jnp.int32, hp_ref.shape, 0)
        halo = jnp.where((r == 0) & ~last, _rms(hn_ref[...], gain),
                         jnp.where((r == HALO - 1) & ~first, _rms(hp_ref[...], gain), 0.0))
        a_s[tm:rows, :] = halo.astype(BF16)

    a = a_s[...]

    def conv_branch(w_ref, cw_ref, cb_ref):
        up = jnp.dot(a, w_ref[...], preferred_element_type=F32)
        cw = cw_ref[...]
        c = (pltpu.roll(up, 1, 0) * cw[0:1] + up * cw[1:2]
             + pltpu.roll(up, rows - 1, 0) * cw[2:3])
        return c[0:tm] + cb_ref[...]

    g = conv_branch(wg_ref, cwg_ref, cbg_ref)
    u = conv_branch(wu_ref, cwu_ref, cbu_ref)
    half_g = 0.5 * g
    act = ((half_g + half_g * jnp.tanh(half_g)) * u).astype(BF16)
    o_ref[...] += jnp.dot(act, wd_ref[...], preferred_element_type=F32)

    if final_norm:
        @pl.when(j == pl.num_programs(1) - 1)
        def _():
            for c in range(0, tm, NORM_CHUNK):
                o_ref[c:c + NORM_CHUNK, :] = _rms(o_ref[c:c + NORM_CHUNK, :], fg_ref[...])


def _ffn(h, gain, w_up, conv_w, conv_b, w_down, layer, final_gain, side_casts,
         *, final_norm, tm=1024, tn=512):
    t, d = h.shape
    d_ff = w_down.shape[0]
    ni, nj = t // tm, d_ff // tn
    seq_tiles = SEQ // tm
    hb = tm // HALO
    n_hblocks = t // HALO
    in_specs = [
        pl.BlockSpec((tm, d), lambda i, j: (i, 0)),
        pl.BlockSpec((HALO, d), lambda i, j: (jnp.maximum(i * hb - 1, 0), 0)),
        pl.BlockSpec((HALO, d), lambda i, j: (jnp.minimum((i + 1) * hb, n_hblocks - 1), 0)),
        pl.BlockSpec((1, d), lambda i, j: (0, 0)),
        pl.BlockSpec((d, tn), lambda i, j: (0, j)),
        pl.BlockSpec((d, tn), lambda i, j: (0, j + nj)),
        pl.BlockSpec((None, 3, tn), lambda i, j: (layer, 0, j)),
        pl.BlockSpec((None, 3, tn), lambda i, j: (layer, 0, j + nj)),
        pl.BlockSpec((None, 1, tn), lambda i, j: (layer, 0, j)),
        pl.BlockSpec((None, 1, tn), lambda i, j: (layer, 0, j + nj)),
        pl.BlockSpec((tn, d), lambda i, j: (j, 0)),
        pl.BlockSpec((1, d), lambda i, j: (0, 0)),
    ]
    operands = [h, h, h, gain.reshape(1, d), w_up, w_up, conv_w, conv_w, conv_b, conv_b, w_down,
                final_gain.reshape(1, d)]
    out_specs = [pl.BlockSpec((tm, d), lambda i, j: (i, 0))]
    out_shape = [jax.ShapeDtypeStruct((t, d), F32)]
    cast_in, cast_out, cast_shapes, cast_operands, cast_steps = _side_cast_specs(side_casts, ni, nj)
    return pl.pallas_call(
        partial(_ffn_kernel, tm=tm, nj=nj, seq_tiles=seq_tiles, final_norm=final_norm,
                cast_steps=cast_steps),
        grid=(ni, nj),
        in_specs=in_specs + cast_in,
        out_specs=out_specs + cast_out,
        out_shape=out_shape + cast_shapes,
        scratch_shapes=[pltpu.VMEM((tm + HALO, d), BF16)],
        compiler_params=_params("parallel", "arbitrary", vmem=60 * MIB),
        name="conv_glu_ffn",
    )(*operands, *cast_operands)


def _na_bias_table(rpb):
    heads, _, n_dc = rpb.shape
    cols = jnp.arange(GRID_W)
    c_start = jnp.clip(cols - NA_WIN_C // 2, 0, GRID_W - NA_WIN_C)
    col_valid = (cols[None, :] >= c_start[:, None]) & (cols[None, :] < c_start[:, None] + NA_WIN_C)
    offs = cols[None, :] - cols[:, None] + NA_WIN_C - 1
    onehot = (offs[None] == jnp.arange(n_dc)[:, None, None]).astype(F32)
    tbl = jnp.einsum('hdj,jqk->hqdk', rpb * LOG2E, onehot, precision=lax.Precision.HIGHEST)
    tbl = jnp.where(col_valid[None, :, None, :], tbl, NEG_INF).reshape(heads, GRID_W, -1)
    left = NA_PAD_ROWS * GRID_W
    pad = lambda x, l: jnp.pad(x, ((0, 0), (0, 0), (l, NA_TABLE_LANES - l - x.shape[-1])),
                               constant_values=NEG_INF)
    return jnp.stack([pad(tbl, left), pad(tbl, left - GRID_W)], axis=1).astype(F32)


def _with_ones_column(v):
    return jnp.concatenate([v, jnp.ones_like(v)], axis=1)


def _na_kernel(q_ref, k_ref, v_ref, b_ref, o_ref, s_s, p_s, v_s, *, group):
    union_rows = NA_WIN_R + NA_BLOCK_ROWS - 1
    union = union_rows * GRID_W
    bq = NA_BLOCK_ROWS * GRID_W
    n_groups = GRID_ROWS // (NA_BLOCK_ROWS * group)
    lanes = [slice(hd * HEAD_DIM, (hd + 1) * HEAD_DIM) for hd in range(NA_HEADS_PER_STEP)]
    for hd, ls in enumerate(lanes):
        v_s[hd] = _with_ones_column(v_ref[:, ls])

    def r_start(r):
        return min(max(r - NA_WIN_R // 2, 0), GRID_ROWS - NA_WIN_R)

    def k_start(blk):
        return min(r_start(blk * NA_BLOCK_ROWS), GRID_ROWS - union_rows)

    def row_bias(hd, r, start):
        first = NA_WIN_R - 1 + NA_PAD_ROWS - (r - start)
        lane0 = (first // 2) * 2 * GRID_W
        b = b_ref[hd, first % 2, :, lane0:lane0 + union]
        lo, hi = (r_start(r) - start) * GRID_W, (r_start(r) - start + NA_WIN_R) * GRID_W
        tiles = []
        for t0 in range(0, union, LANES):
            t1 = min(t0 + LANES, union)
            tile = b[:, t0:t1]
            if t1 <= lo or t0 >= hi:
                tile = jnp.full_like(tile, NEG_INF)
            elif t0 < lo or t1 > hi:
                lane = t0 + lax.broadcasted_iota(jnp.int32, tile.shape, 1)
                tile = jnp.where((lane >= lo) & (lane < hi), tile, NEG_INF)
            tiles.append(tile)
        return jnp.concatenate(tiles, axis=1)

    def blocks(g):
        return range(g * group, (g + 1) * group)

    def scores(hd, g):
        for blk in blocks(g):
            q0, start = blk * bq, k_start(blk)
            q = (q_ref[q0:q0 + bq, lanes[hd]].astype(F32) * (HEAD_DIM ** -0.5 * LOG2E)).astype(BF16)
            s = lax.dot_general(q, k_ref[start * GRID_W:start * GRID_W + union, lanes[hd]],
                                (((1,), (1,)), ((), ())), preferred_element_type=F32)
            bias = jnp.concatenate([row_bias(hd, blk * NA_BLOCK_ROWS + i, start)
                                    for i in range(NA_BLOCK_ROWS)], axis=0)
            s_s[hd, q0:q0 + bq, :] = s + bias

    def softmax(hd, g):
        rs = slice(g * group * bq, (g + 1) * group * bq)
        s = s_s[hd, rs, :]
        p_s[hd, rs, :] = jnp.exp2(s - jnp.max(s, axis=-1, keepdims=True)).astype(BF16)

    def weighted_values(hd, g):
        for blk in blocks(g):
            q0, k0 = blk * bq, k_start(blk) * GRID_W
            o = jnp.dot(p_s[hd, q0:q0 + bq, :], v_s[hd, k0:k0 + union, :],
                        preferred_element_type=F32)
            o_ref[q0:q0 + bq, lanes[hd]] = (
                o[:, :HEAD_DIM] / o[:, HEAD_DIM:HEAD_DIM + 1]).astype(o_ref.dtype)

    stages = [(hd, g) for hd in range(NA_HEADS_PER_STEP) for g in range(n_groups)]
    scores(*stages[0])
    for n, stage in enumerate(stages):
        if n + 1 < len(stages):
            scores(*stages[n + 1])
        softmax(*stage)
        weighted_values(*stage)


def _na_attention(qkv, bias_tbl, batch, *, group=4):
    t = qkv.shape[0]
    union = (NA_WIN_R + NA_BLOCK_ROWS - 1) * GRID_W
    hps = NA_HEADS_PER_STEP
    n_steps = NA_HEADS // hps
    blk = lambda off: pl.BlockSpec((SEQ, hps * HEAD_DIM), lambda h, b: (b, h + off))
    return pl.pallas_call(
        partial(_na_kernel, group=group),
        grid=(n_steps, batch),
        in_specs=[
            blk(0), blk(n_steps), blk(2 * n_steps),
            pl.BlockSpec((hps, 2, GRID_W, NA_TABLE_LANES), lambda h, b: (h, 0, 0, 0)),
        ],
        out_specs=pl.BlockSpec((SEQ, hps * HEAD_DIM), lambda h, b: (b, h)),
        out_shape=jax.ShapeDtypeStruct((t, NA_HEADS * HEAD_DIM), BF16),
        scratch_shapes=[pltpu.VMEM((hps, SEQ, union), F32), pltpu.VMEM((hps, SEQ, union), BF16),
                        pltpu.VMEM((hps, SEQ, 2 * HEAD_DIM), BF16)],
        compiler_params=_params("parallel", "arbitrary"),
        name="neighborhood_attention",
    )(qkv, qkv, qkv, bias_tbl)


def _sc_out_kernel(gb_ref, gc_ref, hh_ref, gcp_ref, hhp_ref, gcn_ref, hhn_ref, cw_ref,
                   w_ref, r_ref, o_ref, *, tm, seq_tiles, chunk):
    i = pl.program_id(0)
    first = (i % seq_tiles) == 0
    last = (i % seq_tiles) == seq_tiles - 1
    rows = lax.broadcasted_iota(jnp.int32, (tm, chunk), 0)
    acc = r_ref[...]
    for c in range(0, w_ref.shape[0], chunk):
        cs = slice(c, c + chunk)
        p = gc_ref[:, cs].astype(F32) * hh_ref[:, cs].astype(F32)
        prev = gcp_ref[HALO - 1:HALO, cs].astype(F32) * hhp_ref[HALO - 1:HALO, cs].astype(F32)
        nxt = gcn_ref[0:1, cs].astype(F32) * hhn_ref[0:1, cs].astype(F32)
        prev = jnp.where(first, 0.0, prev)
        nxt = jnp.where(last, 0.0, nxt)
        pm1 = jnp.where(rows == 0, prev, pltpu.roll(p, 1, 0))
        pp1 = jnp.where(rows == tm - 1, nxt, pltpu.roll(p, tm - 1, 0))
        cw = cw_ref[:, cs]
        conv = pm1 * cw[0:1] + p * cw[1:2] + pp1 * cw[2:3]
        y = (gb_ref[:, cs].astype(F32) * conv).astype(BF16)
        acc = acc + jnp.dot(y, w_ref[cs, :], preferred_element_type=F32)
    o_ref[...] = acc


def _sc_out(z, conv_w, w_out, resid, *, tm=512, chunk=512):
    t = z.shape[0]
    d, n = w_out.shape
    seq_tiles = SEQ // tm
    hb = tm // HALO
    n_hblocks = t // HALO
    main = lambda c: pl.BlockSpec((tm, d), lambda i: (i, c))
    prev = lambda c: pl.BlockSpec((HALO, d), lambda i: (jnp.maximum(i * hb - 1, 0), c))
    nxt = lambda c: pl.BlockSpec((HALO, d), lambda i: (jnp.minimum((i + 1) * hb, n_hblocks - 1), c))
    return pl.pallas_call(
        partial(_sc_out_kernel, tm=tm, seq_tiles=seq_tiles, chunk=chunk),
        grid=(t // tm,),
        in_specs=[
            main(0), main(1), main(2), prev(1), prev(2), nxt(1), nxt(2),
            pl.BlockSpec((3, d), lambda i: (0, 0)),
            pl.BlockSpec((d, n), lambda i: (0, 0)),
            pl.BlockSpec((tm, n), lambda i: (i, 0)),
        ],
        out_specs=pl.BlockSpec((tm, n), lambda i: (i, 0)),
        out_shape=jax.ShapeDtypeStruct((t, n), F32),
        compiler_params=_params("parallel"),
        name="short_conv_out",
    )(z, z, z, z, z, z, z, conv_w, w_out, resid)


def _rope_tables():
    t = jnp.arange(SEQ)
    row = (t // GRID_W).astype(F32)[:, None]
    col = (t % GRID_W).astype(F32)[:, None]
    half = HEAD_DIM // 2
    inv = ROPE_THETA ** (-jnp.arange(0, half, 2, dtype=F32) / half)
    ang = jnp.concatenate([row * inv, row * inv, col * inv, col * inv], axis=-1)
    cos, sin = jnp.cos(ang), jnp.sin(ang)
    lane = jnp.arange(HEAD_DIM)[None, :]
    lower = (lane % half) < (half // 2)
    sin_lo = jnp.where(lower, -sin, 0.0)
    sin_hi = jnp.where(lower, 0.0, sin)
    return cos, sin_lo, sin_hi


def _norm_rope(x, gain, cos, sin_lo, sin_hi):
    y = _rms(x, gain)
    quarter = HEAD_DIM // 4
    return (y * cos + pltpu.roll(y, HEAD_DIM - quarter, 1) * sin_lo
            + pltpu.roll(y, quarter, 1) * sin_hi)


def _gqa_kernel(q_ref, k_ref, v_ref, qn_ref, kn_ref, cq_ref, slq_ref, shq_ref,
                ck_ref, slk_ref, shk_ref, o_ref, kt_s, v_s, *, row_parts):
    @pl.when(pl.program_id(2) == 0)
    def _():
        k = _norm_rope(k_ref[...].astype(F32), kn_ref[...], ck_ref[...], slk_ref[...], shk_ref[...])
        kt_s[...] = k.astype(BF16)
        v_s[...] = _with_ones_column(v_ref[...])

    cos, sin_lo, sin_hi = cq_ref[...], slq_ref[...], shq_ref[...]
    tq = q_ref.shape[0]
    units = [(slice(r, r + tq // row_parts), slice(g * HEAD_DIM, (g + 1) * HEAD_DIM))
             for g in range(GQA_GROUP) for r in range(0, tq, tq // row_parts)]

    def scores(rs, hs):
        q = _norm_rope(q_ref[rs, hs].astype(F32), qn_ref[...], cos[rs], sin_lo[rs], sin_hi[rs])
        q = (q * (HEAD_DIM ** -0.5 * LOG2E)).astype(BF16)
        return lax.dot_general(q, kt_s[...], (((1,), (1,)), ((), ())),
                               preferred_element_type=F32)

    def weighted_values(rs, hs, s):
        p = jnp.exp2(s - jnp.max(s, axis=-1, keepdims=True)).astype(BF16)
        o = jnp.dot(p, v_s[...], preferred_element_type=F32)
        o_ref[rs, hs] = (o[:, :HEAD_DIM] / o[:, HEAD_DIM:HEAD_DIM + 1]).astype(o_ref.dtype)

    s_next = scores(*units[0])
    for n, unit in enumerate(units):
        s = s_next
        if n + 1 < len(units):
            s_next = scores(*units[n + 1])
        weighted_values(*unit, s)


def _gqa_attention(qkv, q_norm, k_norm, batch, *, tq=1024, row_parts=4):
    t = qkv.shape[0]
    nq = SEQ // tq
    gw = GQA_GROUP * HEAD_DIM
    cos, sin_lo, sin_hi = _rope_tables()
    qtab = pl.BlockSpec((tq, HEAD_DIM), lambda b, h, i: (i, 0))
    ktab = pl.BlockSpec((SEQ, HEAD_DIM), lambda b, h, i: (0, 0))
    vec = pl.BlockSpec((1, HEAD_DIM), lambda b, h, i: (0, 0))
    return pl.pallas_call(
        partial(_gqa_kernel, row_parts=row_parts),
        grid=(batch, GQA_KV_HEADS, nq),
        in_specs=[
            pl.BlockSpec((tq, gw), lambda b, h, i: (b * nq + i, h)),
            pl.BlockSpec((SEQ, HEAD_DIM), lambda b, h, i: (b, GQA_Q_HEADS + h)),
            pl.BlockSpec((SEQ, HEAD_DIM), lambda b, h, i: (b, GQA_Q_HEADS + GQA_KV_HEADS + h)),
            vec, vec, qtab, qtab, qtab, ktab, ktab, ktab,
        ],
        out_specs=pl.BlockSpec((tq, gw), lambda b, h, i: (b * nq + i, h)),
        out_shape=jax.ShapeDtypeStruct((t, GQA_Q_HEADS * HEAD_DIM), BF16),
        scratch_shapes=[pltpu.VMEM((SEQ, HEAD_DIM), BF16), pltpu.VMEM((SEQ, 2 * HEAD_DIM), BF16)],
        compiler_params=_params("parallel", "parallel", "arbitrary"),
        name="gqa_attention",
    )(qkv, qkv, qkv, q_norm.reshape(1, -1), k_norm.reshape(1, -1),
      cos, sin_lo, sin_hi, cos, sin_lo, sin_hi)


def kernel(x, mix_norm, ffn_norm, final_norm, na_w_qkv, na_rpb, na_w_o, sc_w_in, sc_conv_w,
           sc_w_out, gqa_w_qkv, gqa_q_norm, gqa_k_norm, gqa_w_o, ffn_w_up, ffn_conv_w,
           ffn_conv_b, ffn_w_down):
    batch, seq, d = x.shape
    assert (seq, d) == (SEQ, D_MODEL)
    depth = mix_norm.shape[0]
    h = x.reshape(batch * seq, d)
    mixer_f32 = ((na_w_qkv, na_w_o), (sc_w_in, sc_w_out), (gqa_w_qkv, gqa_w_o))

    def layer_weights_f32(i):
        return [(w, i // N_MIXERS) for w in mixer_f32[i % N_MIXERS]] + [(ffn_w_up, i), (ffn_w_down, i)]

    w_in, w_out = [w[l].astype(BF16) for w, l in layer_weights_f32(0)[:2]]
    w_up = w_down = None
    ffn_conv_b = ffn_conv_b.reshape(depth, 1, -1)
    for i in range(depth):
        m, j = i % N_MIXERS, i // N_MIXERS
        first_casts = layer_weights_f32(0)[2:] if i == 0 else ()
        if m == 0:
            qkv, *cast = _norm_matmul(h, mix_norm[i], w_in, first_casts)
            att = _na_attention(qkv, _na_bias_table(na_rpb[j]), batch)
            h = _matmul_resid(att, w_out, h)
        elif m == 1:
            z, *cast = _norm_matmul(h, mix_norm[i], w_in, first_casts)
            h = _sc_out(z, sc_conv_w[j], w_out, h)
        else:
            qkv, *cast = _norm_matmul(h, mix_norm[i], w_in, first_casts)
            att = _gqa_attention(qkv, gqa_q_norm[j], gqa_k_norm[j], batch)
            h = _matmul_resid(att, w_out, h)
        if cast:
            w_up, w_down = cast
        side_casts = layer_weights_f32(i + 1) if i + 1 < depth else []
        h, *next_weights = _ffn(h, ffn_norm[i], w_up, ffn_conv_w, ffn_conv_b, w_down, i, final_norm,
                                side_casts, final_norm=(i == depth - 1))
        if next_weights:
            w_in, w_out, w_up, w_down = next_weights
    return h.reshape(batch, seq, d)
```
